```python
import jax, jax.numpy as jnp
from jax import lax
import numpy as np

D_MODEL = 1024
BATCH = 16
SEQ = 2048
DEPTH = 4

CHUNK = 64
N_MIXERS = 3
HEAD_DIM = 64
N_HEADS = D_MODEL // HEAD_DIM
Q_BLOCK = 128
SG_CHUNK = 128
SG_GROUPS = 8
SG_HALF = 3 * D_MODEL
CA_PREV = 8
CA_BAND = (CA_PREV + 1) * CHUNK
REL_CLIP = 128
D_FF = 2816
N_EXPERTS = 8
TOP_K = 2
D_FF_EXPERT = 3584
EPS = 1e-6

N_SB = (DEPTH + 2) // 3
N_SG = (DEPTH + 1) // 3
N_CA = DEPTH // 3
N_DENSE = (DEPTH + 1) // 2
N_MOE = DEPTH // 2

kernel_name = 'hybrid_chunk_causal_sb_sgu_relattn_moe'


def rms_norm(x, g):
    xf = x.astype(jnp.float32)
    y = xf * lax.rsqrt(jnp.mean(xf * xf, axis=-1, keepdims=True) + EPS)
    return (y * g.astype(jnp.float32)).astype(x.dtype)


def stick_breaking_mixer(h, w_qkv, q_g, k_g, w_o):
    b, s, _ = h.shape
    qkv = (h @ w_qkv).reshape(b, s, 3, N_HEADS, HEAD_DIM)
    q = rms_norm(qkv[:, :, 0], q_g)
    k = rms_norm(qkv[:, :, 1], k_g)
    v = qkv[:, :, 2]
    scale = HEAD_DIM ** -0.5
    outs = []
    for blk in range(s // Q_BLOCK):
        q0 = blk * Q_BLOCK
        kn = q0 + Q_BLOCK
        z = jnp.einsum('bqhd,bkhd->bhqk', q[:, q0:kn], k[:, :kn]).astype(jnp.float32) * scale
        qpos = q0 + jnp.arange(Q_BLOCK)[:, None]
        kpos = jnp.arange(kn)[None, :]
        strict = kpos < qpos
        log_fail = jnp.where(strict, jax.nn.log_sigmoid(-z), 0.0)
        after = lax.cumsum(log_fail, axis=3, reverse=True) - log_fail
        a = jnp.where(strict, jnp.exp(jax.nn.log_sigmoid(z) + after), 0.0)
        outs.append(jnp.einsum('bhqk,bkhd->bqhd', a.astype(v.dtype), v[:, :kn]))
    o = jnp.concatenate(outs, axis=1).reshape(b, s, D_MODEL)
    return o @ w_o


def spatial_gating_mixer(h, w_in, v_g, w_s, b_s, w_out):
    b, s, _ = h.shape
    z = jax.nn.gelu(h @ w_in)
    u = z[..., :SG_HALF]
    v = rms_norm(z[..., SG_HALF:], v_g)
    n = s // SG_CHUNK
    vc = v.reshape(b, n, SG_CHUNK, SG_GROUPS, SG_HALF // SG_GROUPS)
    pos = jnp.arange(SG_CHUNK)
    mask = (pos[None, :] // CHUNK) <= (pos[:, None] // CHUNK)
    ws = jnp.where(mask[None], w_s, 0.0)
    mixed = jnp.einsum('gij,bnjgc->bnigc', ws.astype(vc.dtype), vc) + b_s.T[None, None, :, :, None]
    y = u * mixed.reshape(b, s, SG_HALF)
    return y @ w_out


def chunk_attention_mixer(h, w_qkv, q_g, k_g, rel_bias, w_o):
    b, s, _ = h.shape
    qkv = (h @ w_qkv).reshape(b, s, 3, N_HEADS, HEAD_DIM)
    q = rms_norm(qkv[:, :, 0], q_g) * (HEAD_DIM ** -0.5)
    k = rms_norm(qkv[:, :, 1], k_g)
    v = qkv[:, :, 2]
    pad = CA_PREV * CHUNK
    kp = jnp.pad(k, ((0, 0), (pad, 0), (0, 0), (0, 0)))
    vp = jnp.pad(v, ((0, 0), (pad, 0), (0, 0), (0, 0)))
    qi = jnp.arange(CHUNK)[:, None]
    kj = jnp.arange(CA_BAND)[None, :]
    rel = qi + pad - kj
    idx = jnp.clip(rel, -REL_CLIP, REL_CLIP) + REL_CLIP
    bias = rel_bias[:, idx].astype(jnp.float32)

    def one_chunk(n):
        start = n * CHUNK
        qn = lax.dynamic_slice_in_dim(q, start, CHUNK, axis=1)
        kn = lax.dynamic_slice_in_dim(kp, start, CA_BAND, axis=1)
        vn = lax.dynamic_slice_in_dim(vp, start, CA_BAND, axis=1)
        logits = jnp.einsum('bqhd,bkhd->bhqk', qn, kn).astype(jnp.float32) + bias
        valid = (start - pad + kj) >= 0
        logits = jnp.where(valid, logits, -jnp.inf)
        p = jax.nn.softmax(logits, axis=-1)
        return jnp.einsum('bhqk,bkhd->bqhd', p.astype(vn.dtype), vn)

    o = lax.map(one_chunk, jnp.arange(s // CHUNK))
    o = jnp.moveaxis(o, 0, 1).reshape(b, s, D_MODEL)
    return o @ w_o


def swiglu(h, w13, w2):
    a, g = jnp.split(h @ w13, 2, axis=-1)
    return (jax.nn.silu(a) * g) @ w2


def moe_swiglu(h, w_r, b_r, w1, w3, w2):
    logits = (h @ w_r).astype(jnp.float32) + b_r.astype(jnp.float32)
    top_v, top_i = lax.top_k(logits, TOP_K)
    top_w = jax.nn.softmax(top_v, axis=-1)
    gates = jnp.sum(jax.nn.one_hot(top_i, N_EXPERTS, dtype=jnp.float32) * top_w[..., None], axis=-2)
    out = jnp.zeros_like(h)
    for e in range(N_EXPERTS):
        y = (jax.nn.silu(h @ w1[e]) * (h @ w3[e])) @ w2[e]
        out = out + gates[..., e:e + 1].astype(h.dtype) * y
    return out


def setup_inputs(seed: int = 0) -> dict:
    key = jax.random.key(seed)
    ks = jax.random.split(key, 26)
    nrm = jax.random.normal
    f32 = jnp.float32
    d = D_MODEL
    return {
        'x': nrm(ks[0], (BATCH, SEQ, d), f32),
        'c': nrm(ks[1], (BATCH, d), f32),
        'norm_g': 1.0 + 0.02 * nrm(ks[2], (DEPTH, 2, d), f32),
        'ada_w': nrm(ks[3], (DEPTH, d, 6 * d), f32) * (0.5 * d ** -0.5),
        'ada_b': 0.02 * nrm(ks[4], (DEPTH, 6 * d), f32),
        'sb_wqkv': nrm(ks[5], (N_SB, d, 3 * d), f32) * d ** -0.5,
        'sb_qg': 1.0 + 0.02 * nrm(ks[6], (N_SB, HEAD_DIM), f32),
        'sb_kg': 1.0 + 0.02 * nrm(ks[7], (N_SB, HEAD_DIM), f32),
        'sb_wo': nrm(ks[8], (N_SB, d, d), f32) * d ** -0.5,
        'sg_win': nrm(ks[9], (N_SG, d, 2 * SG_HALF), f32) * d ** -0.5,
        'sg_vg': 1.0 + 0.02 * nrm(ks[10], (N_SG, SG_HALF), f32),
        'sg_ws': nrm(ks[11], (N_SG, SG_GROUPS, SG_CHUNK, SG_CHUNK), f32) * SG_CHUNK ** -0.5,
        'sg_bs': 1.0 + 0.02 * nrm(ks[12], (N_SG, SG_GROUPS, SG_CHUNK), f32),
        'sg_wout': nrm(ks[13], (N_SG, SG_HALF, d), f32) * SG_HALF ** -0.5,
        'ca_wqkv': nrm(ks[14], (N_CA, d, 3 * d), f32) * d ** -0.5,
        'ca_qg': 1.0 + 0.02 * nrm(ks[15], (N_CA, HEAD_DIM), f32),
        'ca_kg': 1.0 + 0.02 * nrm(ks[16], (N_CA, HEAD_DIM), f32),
        'ca_relb': 0.2 * nrm(ks[17], (N_CA, N_HEADS, 2 * REL_CLIP + 1), f32),
        'ca_wo': nrm(ks[18], (N_CA, d, d), f32) * d ** -0.5,
        'ff_w13': nrm(ks[19], (N_DENSE, d, 2 * D_FF), f32) * d ** -0.5,
        'ff_w2': nrm(ks[20], (N_DENSE, D_FF, d), f32) * D_FF ** -0.5,
        'moe_wr': nrm(ks[21], (N_MOE, d, N_EXPERTS), f32) * d ** -0.5,
        'moe_br': 0.01 * nrm(ks[22], (N_MOE, N_EXPERTS), f32),
        'moe_w1': nrm(ks[23], (N_MOE, N_EXPERTS, d, D_FF_EXPERT), f32) * d ** -0.5,
        'moe_w3': nrm(ks[24], (N_MOE, N_EXPERTS, d, D_FF_EXPERT), f32) * d ** -0.5,
        'moe_w2': nrm(ks[25], (N_MOE, N_EXPERTS, D_FF_EXPERT, d), f32) * D_FF_EXPERT ** -0.5,
    }


def reference(x, c, norm_g, ada_w, ada_b, sb_wqkv, sb_qg, sb_kg, sb_wo, sg_win, sg_vg, sg_ws, sg_bs,
              sg_wout, ca_wqkv, ca_qg, ca_kg, ca_relb, ca_wo, ff_w13, ff_w2, moe_wr, moe_br,
              moe_w1, moe_w3, moe_w2):
    cond = jax.nn.silu(c)
    for i in range(DEPTH):
        mod = (cond @ ada_w[i] + ada_b[i])[:, None, :]
        sh1, sc1, g1, sh2, sc2, g2 = jnp.split(mod, 6, axis=-1)
        h = rms_norm(x, norm_g[i, 0]) * (1 + sc1) + sh1
        kind = i % N_MIXERS
        j = i // N_MIXERS
        if kind == 0:
            y = stick_breaking_mixer(h, sb_wqkv[j], sb_qg[j], sb_kg[j], sb_wo[j])
        elif kind == 1:
            y = spatial_gating_mixer(h, sg_win[j], sg_vg[j], sg_ws[j], sg_bs[j], sg_wout[j])
        else:
            y = chunk_attention_mixer(h, ca_wqkv[j], ca_qg[j], ca_kg[j], ca_relb[j], ca_wo[j])
        x = x + g1 * y
        h = rms_norm(x, norm_g[i, 1]) * (1 + sc2) + sh2
        m = i // 2
        if i % 2 == 0:
            y = swiglu(h, ff_w13[m], ff_w2[m])
        else:
            y = moe_swiglu(h, moe_wr[m], moe_br[m], moe_w1[m], moe_w3[m], moe_w2[m])
        x = x + g2 * y
    return x
```

```python
import functools

import jax
import jax.numpy as jnp
from jax import lax
from jax.experimental import pallas as pl
from jax.experimental.pallas import tpu as pltpu

F32 = jnp.float32
BF16 = jnp.bfloat16

EPS = 1e-6
HEAD_DIM = 64
LANES = 128
CHUNK = 64
SG_CHUNK = 128
CA_PREV = 8
REL_CLIP = 128
N_EXPERTS = 8
NEG_INF = float("-inf")

VMEM_LIMIT_BYTES = 56 * 1024 * 1024


def _params(*semantics):
    return pltpu.CompilerParams(dimension_semantics=semantics, vmem_limit_bytes=VMEM_LIMIT_BYTES)


def _norm_mod(x, g, scale, shift):
    ms = jnp.mean(x * x, axis=-1, keepdims=True)
    y = x * lax.rsqrt(ms + EPS) * g
    return y * (1.0 + scale) + shift


def _head_rms_norm(x, g2):
    lane = lax.broadcasted_iota(jnp.int32, x.shape, 1)
    first = lane < HEAD_DIM
    x2 = x * x
    s0 = jnp.sum(jnp.where(first, x2, 0.0), axis=-1, keepdims=True)
    s1 = jnp.sum(jnp.where(first, 0.0, x2), axis=-1, keepdims=True)
    ms = jnp.where(first, s0, s1) * (1.0 / HEAD_DIM)
    return x * lax.rsqrt(ms + EPS) * g2


def _ada_kernel(c_ref, w_ref, b_ref, o_ref):
    c = c_ref[...]
    cond = c * jax.nn.sigmoid(c)
    o_ref[0] = jnp.dot(cond, w_ref[0], precision=lax.Precision.HIGHEST,
                       preferred_element_type=F32) + b_ref[0]


def ada_mods(c, ada_w, ada_b, *, tn=1536):
    depth, d, n = ada_w.shape
    b = c.shape[0]
    return pl.pallas_call(
        _ada_kernel,
        out_shape=jax.ShapeDtypeStruct((depth, b, n), F32),
        grid=(depth, n // tn),
        in_specs=[
            pl.BlockSpec((b, d), lambda l, j: (0, 0)),
            pl.BlockSpec((1, d, tn), lambda l, j: (l, 0, j)),
            pl.BlockSpec((1, 1, tn), lambda l, j: (l, 0, j)),
        ],
        out_specs=pl.BlockSpec((1, b, tn), lambda l, j: (l, 0, j)),
        compiler_params=_params("arbitrary", "arbitrary"),
        name="ada_mods",
    )(c, ada_w, ada_b.reshape(depth, 1, n))


def _proj_kernel(x_ref, g_ref, mod_ref, w_ref, o_ref, *, shift_idx, scale_idx, nc):
    h = _norm_mod(x_ref[...], g_ref[...], mod_ref[0, scale_idx:scale_idx + 1, :],
                  mod_ref[0, shift_idx:shift_idx + 1, :]).astype(BF16)
    n = o_ref.shape[1]
    for c0 in range(0, n, nc):
        o_ref[:, c0:c0 + nc] = jnp.dot(h, w_ref[:, c0:c0 + nc],
                                       preferred_element_type=F32).astype(BF16)


def norm_proj(x, g, mod, w, *, seq, shift_idx, scale_idx, tm=512):
    t, d = x.shape
    n = w.shape[1]
    per_seq = seq // tm
    return pl.pallas_call(
        functools.partial(_proj_kernel, shift_idx=shift_idx, scale_idx=scale_idx, nc=1024),
        out_shape=jax.ShapeDtypeStruct((t, n), BF16),
        grid=(t // tm,),
        in_specs=[
            pl.BlockSpec((tm, d), lambda i: (i, 0)),
            pl.BlockSpec((1, d), lambda i: (0, 0)),
            pl.BlockSpec((1, 6, d), lambda i: (i // per_seq, 0, 0)),
            pl.BlockSpec((d, n), lambda i: (0, 0)),
        ],
        out_specs=pl.BlockSpec((tm, n), lambda i: (i, 0)),
        compiler_params=_params("parallel"),
        name="norm_proj",
    )(x, g, mod, w)


def _out_proj_kernel(o_ref, w_ref, x_ref, mod_ref, y_ref, *, gate_idx):
    y = jnp.dot(o_ref[...], w_ref[...], preferred_element_type=F32)
    y_ref[...] = x_ref[...] + mod_ref[0, gate_idx:gate_idx + 1, :] * y


def out_proj_residual(o, w, x, mod, *, seq, gate_idx, tm=1024):
    t, d = x.shape
    k = o.shape[1]
    per_seq = seq // tm
    return pl.pallas_call(
        functools.partial(_out_proj_kernel, gate_idx=gate_idx),
        out_shape=jax.ShapeDtypeStruct((t, d), F32),
        grid=(t // tm,),
        in_specs=[
            pl.BlockSpec((tm, k), lambda i: (i, 0)),
            pl.BlockSpec((k, d), lambda i: (0, 0)),
            pl.BlockSpec((tm, d), lambda i: (i, 0)),
            pl.BlockSpec((1, 6, d), lambda i: (i // per_seq, 0, 0)),
        ],
        out_specs=pl.BlockSpec((tm, d), lambda i: (i, 0)),
        compiler_params=_params("parallel"),
        name="out_proj_residual",
    )(o, w, x, mod)


def _sb_kernel(q_ref, k_ref, v_ref, qg_ref, kg_ref, tri_ref, o_ref,
               kn_ref, acc_ref, carry_ref, *, tq, tk, scale):
    i = pl.program_id(2)

    @pl.when(i == 0)
    def _():
        kn_ref[...] = _head_rms_norm(k_ref[...].astype(F32), kg_ref[...]).astype(BF16)

    qn = _head_rms_norm(q_ref[...].astype(F32), qg_ref[...]) * scale
    lane_q = lax.broadcasted_iota(jnp.int32, qn.shape, 1)
    q_heads = (jnp.where(lane_q < HEAD_DIM, qn, 0.0).astype(BF16),
               jnp.where(lane_q < HEAD_DIM, 0.0, qn).astype(BF16))
    tri = tri_ref[...]

    acc_ref[...] = jnp.zeros_like(acc_ref)
    carry_ref[...] = jnp.zeros_like(carry_ref)

    def block(j, diag):
        k0 = pl.multiple_of(j * tk, tk)
        kb = kn_ref[pl.ds(k0, tk), :]
        vb = v_ref[pl.ds(k0, tk), :]
        lane_v = lax.broadcasted_iota(jnp.int32, vb.shape, 1)
        v_heads = (jnp.where(lane_v < HEAD_DIM, vb, jnp.zeros_like(vb)),
                   jnp.where(lane_v < HEAD_DIM, jnp.zeros_like(vb), vb))
        if diag:
            qpos = lax.broadcasted_iota(jnp.int32, (tq, tk), 0)
            kpos = lax.broadcasted_iota(jnp.int32, (tq, tk), 1)
            strict = kpos < qpos
        out = acc_ref[...]
        for hd in range(2):
            z = lax.dot_general(q_heads[hd], kb, (((1,), (1,)), ((), ())),
                                preferred_element_type=F32)
            lf = -(jnp.maximum(z, 0.0) + jnp.log(1.0 + jnp.exp(-jnp.abs(z))))
            if diag:
                lf = jnp.where(strict, lf, 0.0)
            hi = lf.astype(BF16)
            lo = (lf - hi.astype(F32)).astype(BF16)
            after = (jnp.dot(hi, tri, preferred_element_type=F32)
                     + jnp.dot(lo, tri, preferred_element_type=F32))
            carry = carry_ref[hd]
            a = jnp.exp(z + lf + after + carry)
            if diag:
                a = jnp.where(strict, a, 0.0)
            out = out + jnp.dot(a.astype(BF16), v_heads[hd], preferred_element_type=F32)
            carry_ref[hd] = carry + jnp.sum(lf, axis=-1, keepdims=True)
        acc_ref[...] = out

    block(i, True)

    def body(t, c):
        block(i - 1 - t, False)
        return c

    lax.fori_loop(0, i, body, 0)
    o_ref[...] = acc_ref[...].astype(BF16)


def sb_attention(qkv, q_g, k_g, *, batch, seq, tq=256):
    t, n3 = qkv.shape
    d = n3 // 3
    pairs = d // LANES
    tk = tq
    nq = seq // tq
    idx = jnp.arange(tk)
    tri = (idx[:, None] > idx[None, :]).astype(BF16)
    qg2 = jnp.concatenate([q_g, q_g]).reshape(1, LANES)
    kg2 = jnp.concatenate([k_g, k_g]).reshape(1, LANES)
    return pl.pallas_call(
        functools.partial(_sb_kernel, tq=tq, tk=tk, scale=HEAD_DIM ** -0.5),
        out_shape=jax.ShapeDtypeStruct((t, d), BF16),
        grid=(batch, pairs, nq),
        in_specs=[
            pl.BlockSpec((tq, LANES), lambda b, p, i: (b * nq + i, p)),
            pl.BlockSpec((seq, LANES), lambda b, p, i: (b, pairs + p)),
            pl.BlockSpec((seq, LANES), lambda b, p, i: (b, 2 * pairs + p)),
            pl.BlockSpec((1, LANES), lambda b, p, i: (0, 0)),
            pl.BlockSpec((1, LANES), lambda b, p, i: (0, 0)),
            pl.BlockSpec((tk, tk), lambda b, p, i: (0, 0)),
        ],
        out_specs=pl.BlockSpec((tq, LANES), lambda b, p, i: (b * nq + i, p)),
        scratch_shapes=[
            pltpu.VMEM((seq, LANES), BF16),
            pltpu.VMEM((tq, LANES), F32),
            pltpu.VMEM((2, tq, 1), F32),
        ],
        compiler_params=_params("parallel", "parallel", "arbitrary"),
        name="sb_attention",
    )(qkv, qkv, qkv, qg2, kg2, tri)


def _ca_kernel(q_ref, k_ref, v_ref, qg_ref, kg_ref, bias_ref, o_ref, kn_ref, *, tq, nblk, scale):
    i = pl.program_id(2)

    @pl.when(i == 0)
    def _():
        kn_ref[...] = _head_rms_norm(k_ref[...].astype(F32), kg_ref[...]).astype(BF16)

    qn = _head_rms_norm(q_ref[...].astype(F32), qg_ref[...]) * scale
    lane_q = lax.broadcasted_iota(jnp.int32, qn.shape, 1)
    q_heads = (jnp.where(lane_q < HEAD_DIM, qn, 0.0).astype(BF16),
               jnp.where(lane_q < HEAD_DIM, 0.0, qn).astype(BF16))

    kbs, vbs, valid = [], [], []
    for o in range(nblk):
        jb = i - (nblk - 1) + o
        k0 = pl.multiple_of(jnp.maximum(jb, 0) * tq, tq)
        kbs.append(kn_ref[pl.ds(k0, tq), :])
        vbs.append(v_ref[pl.ds(k0, tq), :])
        valid.append(jb >= 0)

    lane_v = lax.broadcasted_iota(jnp.int32, (tq, LANES), 1)
    out = jnp.zeros((tq, LANES), F32)
    for hd in range(2):
        s = []
        for o in range(nblk):
            z = lax.dot_general(q_heads[hd], kbs[o], (((1,), (1,)), ((), ())),
                                preferred_element_type=F32)
            z = z + bias_ref[hd, :, o * tq:(o + 1) * tq]
            s.append(jnp.where(valid[o], z, NEG_INF))
        m = s[0].max(axis=-1, keepdims=True)
        for o in range(1, nblk):
            m = jnp.maximum(m, s[o].max(axis=-1, keepdims=True))
        l = jnp.zeros((tq, 1), F32)
        acc = jnp.zeros((tq, LANES), F32)
        head_lanes = (lane_v < HEAD_DIM) if hd == 0 else (lane_v >= HEAD_DIM)
        for o in range(nblk):
            p = jnp.exp(s[o] - m)
            l = l + jnp.sum(p, axis=-1, keepdims=True)
            vh = jnp.where(head_lanes, vbs[o], jnp.zeros_like(vbs[o]))
            acc = acc + jnp.dot(p.astype(BF16), vh, preferred_element_type=F32)
        out = out + acc / l
    o_ref[...] = out.astype(BF16)


def ca_attention(qkv, q_g, k_g, rel_bias, *, batch, seq, tq=128):
    t, n3 = qkv.shape
    d = n3 // 3
    pairs = d // LANES
    nq = seq // tq
    pad = CA_PREV * CHUNK
    nblk = pad // tq + 1
    band = nblk * tq
    qi = jnp.arange(tq)[:, None]
    kj = jnp.arange(band)[None, :]
    rel = qi + pad - kj
    qc = qi // CHUNK
    kc = (kj - pad) // CHUNK
    visible = (kc <= qc) & (kc >= qc - CA_PREV)
    idx = jnp.clip(rel, -REL_CLIP, REL_CLIP) + REL_CLIP
    bias = jnp.where(visible[None], rel_bias[:, idx].astype(F32), NEG_INF)
    qg2 = jnp.concatenate([q_g, q_g]).reshape(1, LANES)
    kg2 = jnp.concatenate([k_g, k_g]).reshape(1, LANES)
    return pl.pallas_call(
        functools.partial(_ca_kernel, tq=tq, nblk=nblk, scale=HEAD_DIM ** -0.5),
        out_shape=jax.ShapeDtypeStruct((t, d), BF16),
        grid=(batch, pairs, nq),
        in_specs=[
            pl.BlockSpec((tq, LANES), lambda b, p, i: (b * nq + i, p)),
            pl.BlockSpec((seq, LANES), lambda b, p, i: (b, pairs + p)),
            pl.BlockSpec((seq, LANES), lambda b, p, i: (b, 2 * pairs + p)),
            pl.BlockSpec((1, LANES), lambda b, p, i: (0, 0)),
            pl.BlockSpec((1, LANES), lambda b, p, i: (0, 0)),
            pl.BlockSpec((2, tq, band), lambda b, p, i: (p, 0, 0)),
        ],
        out_specs=pl.BlockSpec((tq, LANES), lambda b, p, i: (b * nq + i, p)),
        scratch_shapes=[pltpu.VMEM((seq, LANES), BF16)],
        compiler_params=_params("parallel", "parallel", "arbitrary"),
        name="ca_attention",
    )(qkv, qkv, qkv, qg2, kg2, bias)


def _sgu_in_kernel(x_ref, g_ref, mod_ref, w_ref, vg_ref, o_ref, h_ref, z_ref, *, nc):
    j = pl.program_id(1)
    n = o_ref.shape[1]

    @pl.when(j == 0)
    def _():
        h_ref[...] = _norm_mod(x_ref[...], g_ref[...], mod_ref[0, 1:2, :],
                               mod_ref[0, 0:1, :]).astype(BF16)
        for c0 in range(0, n, nc):
            acc = jnp.dot(h_ref[...], w_ref[:, c0:c0 + nc], preferred_element_type=F32)
            o_ref[:, c0:c0 + nc] = jax.nn.gelu(acc).astype(BF16)

    @pl.when(j == 1)
    def _():
        ss = jnp.zeros((o_ref.shape[0], 1), F32)
        for c0 in range(0, n, nc):
            acc = jnp.dot(h_ref[...], w_ref[:, c0:c0 + nc], preferred_element_type=F32)
            z = jax.nn.gelu(acc)
            z_ref[:, c0:c0 + nc] = z
            ss = ss + jnp.sum(z * z, axis=-1, keepdims=True)
        rstd = lax.rsqrt(ss * (1.0 / n) + EPS)
        o_ref[...] = (z_ref[...] * rstd * vg_ref[...]).astype(BF16)


def sgu_in(x, g, mod, w_in, v_g, *, seq, tm=512):
    t, d = x.shape
    half = w_in.shape[1] // 2
    per_seq = seq // tm
    return pl.pallas_call(
        functools.partial(_sgu_in_kernel, nc=1024),
        out_shape=jax.ShapeDtypeStruct((t, 2 * half), BF16),
        grid=(t // tm, 2),
        in_specs=[
            pl.BlockSpec((tm, d), lambda i, j: (i, 0)),
            pl.BlockSpec((1, d), lambda i, j: (0, 0)),
            pl.BlockSpec((1, 6, d), lambda i, j: (i // per_seq, 0, 0)),
            pl.BlockSpec((d, half), lambda i, j: (0, j)),
            pl.BlockSpec((1, half), lambda i, j: (0, 0)),
        ],
        out_specs=pl.BlockSpec((tm, half), lambda i, j: (i, j)),
        scratch_shapes=[pltpu.VMEM((tm, d), BF16), pltpu.VMEM((tm, half), F32)],
        compiler_params=_params("parallel", "arbitrary"),
        name="sgu_in",
    )(x, g, mod, w_in, v_g)


def _sgu_out_kernel(u_ref, v_ref, ws_ref, bs_ref, wout_ref, x_ref, mod_ref, o_ref, y_ref,
                    *, groups, gw):
    tm = u_ref.shape[0]
    rows = lax.broadcasted_iota(jnp.int32, (SG_CHUNK, SG_CHUNK), 0)
    cols = lax.broadcasted_iota(jnp.int32, (SG_CHUNK, SG_CHUNK), 1)
    causal = (cols // CHUNK) <= (rows // CHUNK)
    for g in range(groups):
        wg = jnp.where(causal, ws_ref[g], 0.0).astype(BF16)
        c0 = g * gw
        for r0 in range(0, tm, SG_CHUNK):
            mixed = jnp.dot(wg, v_ref[r0:r0 + SG_CHUNK, c0:c0 + gw],
                            preferred_element_type=F32) + bs_ref[g]
            u = u_ref[r0:r0 + SG_CHUNK, c0:c0 + gw].astype(F32)
            y_ref[r0:r0 + SG_CHUNK, c0:c0 + gw] = (u * mixed).astype(BF16)
    y = jnp.dot(y_ref[...], wout_ref[...], preferred_element_type=F32)
    o_ref[...] = x_ref[...] + mod_ref[0, 2:3, :] * y


def sgu_out(z, w_s, b_s, w_out, x, mod, *, seq, tm=256):
    t, d = x.shape
    half = z.shape[1] // 2
    groups = w_s.shape[0]
    per_seq = seq // tm
    return pl.pallas_call(
        functools.partial(_sgu_out_kernel, groups=groups, gw=half // groups),
        out_shape=jax.ShapeDtypeStruct((t, d), F32),
        grid=(t // tm,),
        in_specs=[
            pl.BlockSpec((tm, half), lambda i: (i, 0)),
            pl.BlockSpec((tm, half), lambda i: (i, 1)),
            pl.BlockSpec((groups, SG_CHUNK, SG_CHUNK), lambda i: (0, 0, 0)),
            pl.BlockSpec((groups, SG_CHUNK, 1), lambda i: (0, 0, 0)),
            pl.BlockSpec((half, d), lambda i: (0, 0)),
            pl.BlockSpec((tm, d), lambda i: (i, 0)),
            pl.BlockSpec((1, 6, d), lambda i: (i // per_seq, 0, 0)),
        ],
        out_specs=pl.BlockSpec((tm, d), lambda i: (i, 0)),
        scratch_shapes=[pltpu.VMEM((tm, half), BF16)],
        compiler_params=_params("parallel"),
        name="sgu_out",
    )(z, z, w_s, b_s.reshape(groups, SG_CHUNK, 1), w_out, x, mod)


def _ffn_kernel(x_ref, g_ref, mod_ref, w1_ref, w3_ref, w2_ref, o_ref, h_ref, acc_ref):
    j = pl.program_id(1)

    @pl.when(j == 0)
    def _():
        h_ref[...] = _norm_mod(x_ref[...], g_ref[...], mod_ref[0, 4:5, :],
                               mod_ref[0, 3:4, :]).astype(BF16)
        acc_ref[...] = jnp.zeros_like(acc_ref)

    h = h_ref[...]
    a = jnp.dot(h, w1_ref[...], preferred_element_type=F32)
    b = jnp.dot(h, w3_ref[...], preferred_element_type=F32)
    act = (a * jax.nn.sigmoid(a) * b).astype(BF16)
    acc_ref[...] += jnp.dot(act, w2_ref[...], preferred_element_type=F32)

    @pl.when(j == pl.num_programs(1) - 1)
    def _():
        o_ref[...] = x_ref[...] + mod_ref[0, 5:6, :] * acc_ref[...]


def ffn_dense(x, g, mod, w13, w2, *, seq, tm=512, tf=1408):
    t, d = x.shape
    f = w2.shape[0]
    nf = f // tf
    per_seq = seq // tm
    return pl.pallas_call(
        _ffn_kernel,
        out_shape=jax.ShapeDtypeStruct((t, d), F32),
        grid=(t // tm, nf),
        in_specs=[
            pl.BlockSpec((tm, d), lambda i, j: (i, 0)),
            pl.BlockSpec((1, d), lambda i, j: (0, 0)),
            pl.BlockSpec((1, 6, d), lambda i, j: (i // per_seq, 0, 0)),
            pl.BlockSpec((d, tf), lambda i, j: (0, j)),
            pl.BlockSpec((d, tf), lambda i, j: (0, nf + j)),
            pl.BlockSpec((tf, d), lambda i, j: (j, 0)),
        ],
        out_specs=pl.BlockSpec((tm, d), lambda i, j: (i, 0)),
        scratch_shapes=[pltpu.VMEM((tm, d), BF16), pltpu.VMEM((tm, d), F32)],
        compiler_params=_params("parallel", "arbitrary"),
        name="ffn_dense",
    )(x, g, mod, w13, w13, w2)


def _router_kernel(x_ref, g_ref, mod_ref, wr_ref, br_ref, h_ref, gates_ref):
    h = _norm_mod(x_ref[...], g_ref[...], mod_ref[0, 4:5, :], mod_ref[0, 3:4, :])
    h_ref[...] = h.astype(BF16)
    logits = jnp.dot(h, wr_ref[...], precision=lax.Precision.HIGHEST,
                     preferred_element_type=F32) + br_ref[...]
    lane = lax.broadcasted_iota(jnp.int32, logits.shape, 1)
    m1 = jnp.max(logits, axis=-1, keepdims=True)
    i1 = jnp.min(jnp.where(logits == m1, lane, LANES), axis=-1, keepdims=True)
    first = lane == i1
    rest = jnp.where(first, NEG_INF, logits)
    m2 = jnp.max(rest, axis=-1, keepdims=True)
    i2 = jnp.min(jnp.where(rest == m2, lane, LANES), axis=-1, keepdims=True)
    second = lane == i2
    e = jnp.exp(m2 - m1)
    w1 = 1.0 / (1.0 + e)
    gates_ref[...] = jnp.where(first, w1, 0.0) + jnp.where(second, e * w1, 0.0)


def router(x, g, mod, w_r, b_r, *, seq, tm=1024):
    t, d = x.shape
    ne = w_r.shape[1]
    wr = jnp.zeros((d, LANES), F32).at[:, :ne].set(w_r)
    br = jnp.full((1, LANES), NEG_INF, F32).at[0, :ne].set(b_r)
    per_seq = seq // tm
    return pl.pallas_call(
        _router_kernel,
        out_shape=(jax.ShapeDtypeStruct((t, d), BF16), jax.ShapeDtypeStruct((t, LANES), F32)),
        grid=(t // tm,),
        in_specs=[
            pl.BlockSpec((tm, d), lambda i: (i, 0)),
            pl.BlockSpec((1, d), lambda i: (0, 0)),
            pl.BlockSpec((1, 6, d), lambda i: (i // per_seq, 0, 0)),
            pl.BlockSpec((d, LANES), lambda i: (0, 0)),
            pl.BlockSpec((1, LANES), lambda i: (0, 0)),
        ],
        out_specs=(pl.BlockSpec((tm, d), lambda i: (i, 0)),
                   pl.BlockSpec((tm, LANES), lambda i: (i, 0))),
        compiler_params=_params("parallel"),
        name="router",
    )(x, g, mod, wr, br)


def _moe_kernel(h_ref, gates_ref, w1_ref, w3_ref, w2_ref, x_ref, mod_ref, o_ref, acc_ref):
    e = pl.program_id(1)
    j = pl.program_id(2)

    @pl.when((e == 0) & (j == 0))
    def _():
        acc_ref[...] = jnp.zeros_like(acc_ref)

    h = h_ref[...]
    gates = gates_ref[...]
    lane = lax.broadcasted_iota(jnp.int32, gates.shape, 1)
    ge = jnp.sum(jnp.where(lane == e, gates, 0.0), axis=-1, keepdims=True)
    a = jnp.dot(h, w1_ref[0], preferred_element_type=F32)
    b = jnp.dot(h, w3_ref[0], preferred_element_type=F32)
    act = (a * jax.nn.sigmoid(a) * b * ge).astype(BF16)
    acc_ref[...] += jnp.dot(act, w2_ref[0], preferred_element_type=F32)

    @pl.when((e == pl.num_programs(1) - 1) & (j == pl.num_programs(2) - 1))
    def _():
        o_ref[...] = x_ref[...] + mod_ref[0, 5:6, :] * acc_ref[...]


def moe_dense(h, gates, w1, w3, w2, x, mod, *, seq, tm=1024, tf=512):
    t, d = x.shape
    ne, _, f = w1.shape
    per_seq = seq // tm
    return pl.pallas_call(
        _moe_kernel,
        out_shape=jax.ShapeDtypeStruct((t, d), F32),
        grid=(t // tm, ne, f // tf),
        in_specs=[
            pl.BlockSpec((tm, d), lambda i, e, j: (i, 0)),
            pl.BlockSpec((tm, LANES), lambda i, e, j: (i, 0)),
            pl.BlockSpec((1, d, tf), lambda i, e, j: (e, 0, j)),
            pl.BlockSpec((1, d, tf), lambda i, e, j: (e, 0, j)),
            pl.BlockSpec((1, tf, d), lambda i, e, j: (e, j, 0)),
            pl.BlockSpec((tm, d), lambda i, e, j: (i, 0)),
            pl.BlockSpec((1, 6, d), lambda i, e, j: (i // per_seq, 0, 0)),
        ],
        out_specs=pl.BlockSpec((tm, d), lambda i, e, j: (i, 0)),
        scratch_shapes=[pltpu.VMEM((tm, d), F32)],
        compiler_params=_params("parallel", "arbitrary", "arbitrary"),
        name="moe_dense",
    )(h, gates, w1, w3, w2, x, mod)


def kernel(x, c, norm_g, ada_w, ada_b, sb_wqkv, sb_qg, sb_kg, sb_wo, sg_win, sg_vg, sg_ws, sg_bs,
           sg_wout, ca_wqkv, ca_qg, ca_kg, ca_relb, ca_wo, ff_w13, ff_w2, moe_wr, moe_br,
           moe_w1, moe_w3, moe_w2):
    batch, seq, d = x.shape
    depth = norm_g.shape[0]
    mods = ada_mods(c, ada_w, ada_b).reshape(depth, batch, 6, d)
    xt = x.reshape(batch * seq, d)
    bf = lambda w: w.astype(BF16)
    for i in range(depth):
        mod = mods[i]
        kind, j = i % 3, i // 3
        g_tok = norm_g[i, 0].reshape(1, d)
        g_ch = norm_g[i, 1].reshape(1, d)
        if kind == 0:
            qkv = norm_proj(xt, g_tok, mod, bf(sb_wqkv[j]), seq=seq, shift_idx=0, scale_idx=1)
            o = sb_attention(qkv, sb_qg[j], sb_kg[j], batch=batch, seq=seq)
            xt = out_proj_residual(o, bf(sb_wo[j]), xt, mod, seq=seq, gate_idx=2)
        elif kind == 1:
            z = sgu_in(xt, g_tok, mod, bf(sg_win[j]), sg_vg[j].reshape(1, -1), seq=seq)
            xt = sgu_out(z, sg_ws[j], sg_bs[j], bf(sg_wout[j]), xt, mod, seq=seq)
        else:
            qkv = norm_proj(xt, g_tok, mod, bf(ca_wqkv[j]), seq=seq, shift_idx=0, scale_idx=1)
            o = ca_attention(qkv, ca_qg[j], ca_kg[j], ca_relb[j], batch=batch, seq=seq)
            xt = out_proj_residual(o, bf(ca_wo[j]), xt, mod, seq=seq, gate_idx=2)
        m = i // 2
        if i % 2 == 0:
            xt = ffn_dense(xt, g_ch, mod, bf(ff_w13[m]), bf(ff_w2[m]), seq=seq)
        else:
            h, gates = router(xt, g_ch, mod, moe_wr[m], moe_br[m], seq=seq)
            xt = moe_dense(h, gates, bf(moe_w1[m]), bf(moe_w3[m]), bf(moe_w2[m]), xt, mod, seq=seq)
    return xt.reshape(batch, seq, d)
```

```python
import functools

import jax
import jax.numpy as jnp
from jax import lax
from jax.experimental import pallas as pl
from jax.experimental.pallas import tpu as pltpu

F32 = jnp.float32
BF16 = jnp.bfloat16

EPS = 1e-6
HEAD_DIM = 64
LANES = 128
CHUNK = 64
SG_CHUNK = 128
CA_PREV = 8
REL_CLIP = 128
N_EXPERTS = 8
MOE_TM = 512
NEG_INF = float("-inf")

VMEM_LIMIT_BYTES = 56 * 1024 * 1024


def _params(*semantics):
    return pltpu.CompilerParams(dimension_semantics=semantics, vmem_limit_bytes=VMEM_LIMIT_BYTES)


def _norm_mod(x, g, scale, shift):
    ms = jnp.mean(x * x, axis=-1, keepdims=True)
    y = x * lax.rsqrt(ms + EPS) * g
    return y * (1.0 + scale) + shift


def _head_rms_norm(x, g2):
    lane = lax.broadcasted_iota(jnp.int32, x.shape, 1)
    first = lane < HEAD_DIM
    x2 = x * x
    s0 = jnp.sum(jnp.where(first, x2, 0.0), axis=-1, keepdims=True)
    s1 = jnp.sum(jnp.where(first, 0.0, x2), axis=-1, keepdims=True)
    ms = jnp.where(first, s0, s1) * (1.0 / HEAD_DIM)
    return x * lax.rsqrt(ms + EPS) * g2


def _ada_kernel(c_ref, w_ref, b_ref, o_ref):
    c = c_ref[...]
    cond = c * jax.nn.sigmoid(c)
    o_ref[0] = jnp.dot(cond, w_ref[0], precision=lax.Precision.HIGHEST,
                       preferred_element_type=F32) + b_ref[0]


def ada_mods(c, ada_w, ada_b, *, tn=1536):
    depth, d, n = ada_w.shape
    b = c.shape[0]
    return pl.pallas_call(
        _ada_kernel,
        out_shape=jax.ShapeDtypeStruct((depth, b, n), F32),
        grid=(depth, n // tn),
        in_specs=[
            pl.BlockSpec((b, d), lambda l, j: (0, 0)),
            pl.BlockSpec((1, d, tn), lambda l, j: (l, 0, j)),
            pl.BlockSpec((1, 1, tn), lambda l, j: (l, 0, j)),
        ],
        out_specs=pl.BlockSpec((1, b, tn), lambda l, j: (l, 0, j)),
        compiler_params=_params("arbitrary", "arbitrary"),
        name="ada_mods",
    )(c, ada_w, ada_b.reshape(depth, 1, n))


def _proj_kernel(x_ref, g_ref, mod_ref, w_ref, o_ref, *, shift_idx, scale_idx, nc):
    h = _norm_mod(x_ref[...], g_ref[...], mod_ref[0, scale_idx:scale_idx + 1, :],
                  mod_ref[0, shift_idx:shift_idx + 1, :]).astype(BF16)
    n = o_ref.shape[1]
    for c0 in range(0, n, nc):
        o_ref[:, c0:c0 + nc] = jnp.dot(h, w_ref[:, c0:c0 + nc],
                                       preferred_element_type=F32).astype(BF16)


def norm_proj(x, g, mod, w, *, seq, shift_idx, scale_idx, tm=512):
    t, d = x.shape
    n = w.shape[1]
    per_seq = seq // tm
    return pl.pallas_call(
        functools.partial(_proj_kernel, shift_idx=shift_idx, scale_idx=scale_idx, nc=1024),
        out_shape=jax.ShapeDtypeStruct((t, n), BF16),
        grid=(t // tm,),
        in_specs=[
            pl.BlockSpec((tm, d), lambda i: (i, 0)),
            pl.BlockSpec((1, d), lambda i: (0, 0)),
            pl.BlockSpec((1, 6, d), lambda i: (i // per_seq, 0, 0)),
            pl.BlockSpec((d, n), lambda i: (0, 0)),
        ],
        out_specs=pl.BlockSpec((tm, n), lambda i: (i, 0)),
        compiler_params=_params("parallel"),
        name="norm_proj",
    )(x, g, mod, w)


def _out_proj_kernel(o_ref, w_ref, x_ref, mod_ref, y_ref, *, gate_idx):
    y = jnp.dot(o_ref[...], w_ref[...], preferred_element_type=F32)
    y_ref[...] = x_ref[...] + mod_ref[0, gate_idx:gate_idx + 1, :] * y


def out_proj_residual(o, w, x, mod, *, seq, gate_idx, tm=1024):
    t, d = x.shape
    k = o.shape[1]
    per_seq = seq // tm
    return pl.pallas_call(
        functools.partial(_out_proj_kernel, gate_idx=gate_idx),
        out_shape=jax.ShapeDtypeStruct((t, d), F32),
        grid=(t // tm,),
        in_specs=[
            pl.BlockSpec((tm, k), lambda i: (i, 0)),
            pl.BlockSpec((k, d), lambda i: (0, 0)),
            pl.BlockSpec((tm, d), lambda i: (i, 0)),
            pl.BlockSpec((1, 6, d), lambda i: (i // per_seq, 0, 0)),
        ],
        out_specs=pl.BlockSpec((tm, d), lambda i: (i, 0)),
        compiler_params=_params("parallel"),
        name="out_proj_residual",
    )(o, w, x, mod)


def _sb_kernel(q_ref, k_ref, v_ref, qg_ref, kg_ref, tri_ref, o_ref,
               kn_ref, acc_ref, carry_ref, *, tq, tk, scale):
    i = pl.program_id(2)

    @pl.when(i == 0)
    def _():
        kn_ref[...] = _head_rms_norm(k_ref[...].astype(F32), kg_ref[...]).astype(BF16)

    qn = _head_rms_norm(q_ref[...].astype(F32), qg_ref[...]) * scale
    lane_q = lax.broadcasted_iota(jnp.int32, qn.shape, 1)
    q_heads = (jnp.where(lane_q < HEAD_DIM, qn, 0.0).astype(BF16),
               jnp.where(lane_q < HEAD_DIM, 0.0, qn).astype(BF16))
    tri = tri_ref[...]

    acc_ref[...] = jnp.zeros_like(acc_ref)
    carry_ref[...] = jnp.zeros_like(carry_ref)

    def block(j, diag):
        k0 = pl.multiple_of(j * tk, tk)
        kb = kn_ref[pl.ds(k0, tk), :]
        vb = v_ref[pl.ds(k0, tk), :]
        lane_v = lax.broadcasted_iota(jnp.int32, vb.shape, 1)
        v_heads = (jnp.where(lane_v < HEAD_DIM, vb, jnp.zeros_like(vb)),
                   jnp.where(lane_v < HEAD_DIM, jnp.zeros_like(vb), vb))
        if diag:
            qpos = lax.broadcasted_iota(jnp.int32, (tq, tk), 0)
            kpos = lax.broadcasted_iota(jnp.int32, (tq, tk), 1)
            strict = kpos < qpos
        out = acc_ref[...]
        for hd in range(2):
            z = lax.dot_general(q_heads[hd], kb, (((1,), (1,)), ((), ())),
                                preferred_element_type=F32)
            lf = -(jnp.maximum(z, 0.0) + jnp.log(1.0 + jnp.exp(-jnp.abs(z))))
            if diag:
                lf = jnp.where(strict, lf, 0.0)
            hi = lf.astype(BF16)
            lo = (lf - hi.astype(F32)).astype(BF16)
            after = (jnp.dot(hi, tri, preferred_element_type=F32)
                     + jnp.dot(lo, tri, preferred_element_type=F32))
            carry = carry_ref[hd]
            a = jnp.exp(z + lf + after + carry)
            if diag:
                a = jnp.where(strict, a, 0.0)
            out = out + jnp.dot(a.astype(BF16), v_heads[hd], preferred_element_type=F32)
            carry_ref[hd] = carry + jnp.sum(lf, axis=-1, keepdims=True)
        acc_ref[...] = out

    block(i, True)

    def body(t, c):
        block(i - 1 - t, False)
        return c

    lax.fori_loop(0, i, body, 0)
    o_ref[...] = acc_ref[...].astype(BF16)


def sb_attention(qkv, q_g, k_g, *, batch, seq, tq=256):
    t, n3 = qkv.shape
    d = n3 // 3
    pairs = d // LANES
    tk = tq
    nq = seq // tq
    idx = jnp.arange(tk)
    tri = (idx[:, None] > idx[None, :]).astype(BF16)
    qg2 = jnp.concatenate([q_g, q_g]).reshape(1, LANES)
    kg2 = jnp.concatenate([k_g, k_g]).reshape(1, LANES)
    return pl.pallas_call(
        functools.partial(_sb_kernel, tq=tq, tk=tk, scale=HEAD_DIM ** -0.5),
        out_shape=jax.ShapeDtypeStruct((t, d), BF16),
        grid=(batch, pairs, nq),
        in_specs=[
            pl.BlockSpec((tq, LANES), lambda b, p, i: (b * nq + i, p)),
            pl.BlockSpec((seq, LANES), lambda b, p, i: (b, pairs + p)),
            pl.BlockSpec((seq, LANES), lambda b, p, i: (b, 2 * pairs + p)),
            pl.BlockSpec((1, LANES), lambda b, p, i: (0, 0)),
            pl.BlockSpec((1, LANES), lambda b, p, i: (0, 0)),
            pl.BlockSpec((tk, tk), lambda b, p, i: (0, 0)),
        ],
        out_specs=pl.BlockSpec((tq, LANES), lambda b, p, i: (b * nq + i, p)),
        scratch_shapes=[
            pltpu.VMEM((seq, LANES), BF16),
            pltpu.VMEM((tq, LANES), F32),
            pltpu.VMEM((2, tq, 1), F32),
        ],
        compiler_params=_params("parallel", "parallel", "arbitrary"),
        name="sb_attention",
    )(qkv, qkv, qkv, qg2, kg2, tri)


def _ca_kernel(q_ref, k_ref, v_ref, qg_ref, kg_ref, bias_ref, o_ref, kn_ref, *, tq, nblk, scale):
    i = pl.program_id(2)

    @pl.when(i == 0)
    def _():
        kn_ref[...] = _head_rms_norm(k_ref[...].astype(F32), kg_ref[...]).astype(BF16)

    qn = _head_rms_norm(q_ref[...].astype(F32), qg_ref[...]) * scale
    lane_q = lax.broadcasted_iota(jnp.int32, qn.shape, 1)
    q_heads = (jnp.where(lane_q < HEAD_DIM, qn, 0.0).astype(BF16),
               jnp.where(lane_q < HEAD_DIM, 0.0, qn).astype(BF16))

    kbs, vbs, valid = [], [], []
    for o in range(nblk):
        jb = i - (nblk - 1) + o
        k0 = pl.multiple_of(jnp.maximum(jb, 0) * tq, tq)
        kbs.append(kn_ref[pl.ds(k0, tq), :])
        vbs.append(v_ref[pl.ds(k0, tq), :])
        valid.append(jb >= 0)

    lane_v = lax.broadcasted_iota(jnp.int32, (tq, LANES), 1)
    out = jnp.zeros((tq, LANES), F32)
    for hd in range(2):
        s = []
        for o in range(nblk):
            z = lax.dot_general(q_heads[hd], kbs[o], (((1,), (1,)), ((), ())),
                                preferred_element_type=F32)
            z = z + bias_ref[hd, :, o * tq:(o + 1) * tq]
            s.append(jnp.where(valid[o], z, NEG_INF))
        m = s[0].max(axis=-1, keepdims=True)
        for o in range(1, nblk):
            m = jnp.maximum(m, s[o].max(axis=-1, keepdims=True))
        l = jnp.zeros((tq, 1), F32)
        acc = jnp.zeros((tq, LANES), F32)
        head_lanes = (lane_v < HEAD_DIM) if hd == 0 else (lane_v >= HEAD_DIM)
        for o in range(nblk):
            p = jnp.exp(s[o] - m)
            l = l + jnp.sum(p, axis=-1, keepdims=True)
            vh = jnp.where(head_lanes, vbs[o], jnp.zeros_like(vbs[o]))
            acc = acc + jnp.dot(p.astype(BF16), vh, preferred_element_type=F32)
        out = out + acc / l
    o_ref[...] = out.astype(BF16)


def ca_attention(qkv, q_g, k_g, rel_bias, *, batch, seq, tq=128):
    t, n3 = qkv.shape
    d = n3 // 3
    pairs = d // LANES
    nq = seq // tq
    pad = CA_PREV * CHUNK
    nblk = pad // tq + 1
    band = nblk * tq
    qi = jnp.arange(tq)[:, None]
    kj = jnp.arange(band)[None, :]
    rel = qi + pad - kj
    qc = qi // CHUNK
    kc = (kj - pad) // CHUNK
    visible = (kc <= qc) & (kc >= qc - CA_PREV)
    idx = jnp.clip(rel, -REL_CLIP, REL_CLIP) + REL_CLIP
    bias = jnp.where(visible[None], rel_bias[:, idx].astype(F32), NEG_INF)
    qg2 = jnp.concatenate([q_g, q_g]).reshape(1, LANES)
    kg2 = jnp.concatenate([k_g, k_g]).reshape(1, LANES)
    return pl.pallas_call(
        functools.partial(_ca_kernel, tq=tq, nblk=nblk, scale=HEAD_DIM ** -0.5),
        out_shape=jax.ShapeDtypeStruct((t, d), BF16),
        grid=(batch, pairs, nq),
        in_specs=[
            pl.BlockSpec((tq, LANES), lambda b, p, i: (b * nq + i, p)),
            pl.BlockSpec((seq, LANES), lambda b, p, i: (b, pairs + p)),
            pl.BlockSpec((seq, LANES), lambda b, p, i: (b, 2 * pairs + p)),
            pl.BlockSpec((1, LANES), lambda b, p, i: (0, 0)),
            pl.BlockSpec((1, LANES), lambda b, p, i: (0, 0)),
            pl.BlockSpec((2, tq, band), lambda b, p, i: (p, 0, 0)),
        ],
        out_specs=pl.BlockSpec((tq, LANES), lambda b, p, i: (b * nq + i, p)),
        scratch_shapes=[pltpu.VMEM((seq, LANES), BF16)],
        compiler_params=_params("parallel", "parallel", "arbitrary"),
        name="ca_attention",
    )(qkv, qkv, qkv, qg2, kg2, bias)


def _sgu_in_kernel(x_ref, g_ref, mod_ref, w_ref, vg_ref, o_ref, h_ref, z_ref, *, nc):
    j = pl.program_id(1)
    n = o_ref.shape[1]

    @pl.when(j == 0)
    def _():
        h_ref[...] = _norm_mod(x_ref[...], g_ref[...], mod_ref[0, 1:2, :],
                               mod_ref[0, 0:1, :]).astype(BF16)
        for c0 in range(0, n, nc):
            acc = jnp.dot(h_ref[...], w_ref[:, c0:c0 + nc], preferred_element_type=F32)
            o_ref[:, c0:c0 + nc] = jax.nn.gelu(acc).astype(BF16)

    @pl.when(j == 1)
    def _():
        ss = jnp.zeros((o_ref.shape[0], 1), F32)
        for c0 in range(0, n, nc):
            acc = jnp.dot(h_ref[...], w_ref[:, c0:c0 + nc], preferred_element_type=F32)
            z = jax.nn.gelu(acc)
            z_ref[:, c0:c0 + nc] = z
            ss = ss + jnp.sum(z * z, axis=-1, keepdims=True)
        rstd = lax.rsqrt(ss * (1.0 / n) + EPS)
        o_ref[...] = (z_ref[...] * rstd * vg_ref[...]).astype(BF16)


def sgu_in(x, g, mod, w_in, v_g, *, seq, tm=512):
    t, d = x.shape
    half = w_in.shape[1] // 2
    per_seq = seq // tm
    return pl.pallas_call(
        functools.partial(_sgu_in_kernel, nc=1024),
        out_shape=jax.ShapeDtypeStruct((t, 2 * half), BF16),
        grid=(t // tm, 2),
        in_specs=[
            pl.BlockSpec((tm, d), lambda i, j: (i, 0)),
            pl.BlockSpec((1, d), lambda i, j: (0, 0)),
            pl.BlockSpec((1, 6, d), lambda i, j: (i // per_seq, 0, 0)),
            pl.BlockSpec((d, half), lambda i, j: (0, j)),
            pl.BlockSpec((1, half), lambda i, j: (0, 0)),
        ],
        out_specs=pl.BlockSpec((tm, half), lambda i, j: (i, j)),
        scratch_shapes=[pltpu.VMEM((tm, d), BF16), pltpu.VMEM((tm, half), F32)],
        compiler_params=_params("parallel", "arbitrary"),
        name="sgu_in",
    )(x, g, mod, w_in, v_g)


def _sgu_out_kernel(u_ref, v_ref, ws_ref, bs_ref, wout_ref, x_ref, mod_ref, o_ref, y_ref,
                    *, groups, gw):
    tm = u_ref.shape[0]
    rows = lax.broadcasted_iota(jnp.int32, (SG_CHUNK, SG_CHUNK), 0)
    cols = lax.broadcasted_iota(jnp.int32, (SG_CHUNK, SG_CHUNK), 1)
    causal = (cols // CHUNK) <= (rows // CHUNK)
    for g in range(groups):
        wg = jnp.where(causal, ws_ref[g], 0.0).astype(BF16)
        c0 = g * gw
        for r0 in range(0, tm, SG_CHUNK):
            mixed = jnp.dot(wg, v_ref[r0:r0 + SG_CHUNK, c0:c0 + gw],
                            preferred_element_type=F32) + bs_ref[g]
            u = u_ref[r0:r0 + SG_CHUNK, c0:c0 + gw].astype(F32)
            y_ref[r0:r0 + SG_CHUNK, c0:c0 + gw] = (u * mixed).astype(BF16)
    y = jnp.dot(y_ref[...], wout_ref[...], preferred_element_type=F32)
    o_ref[...] = x_ref[...] + mod_ref[0, 2:3, :] * y


def sgu_out(z, w_s, b_s, w_out, x, mod, *, seq, tm=256):
    t, d = x.shape
    half = z.shape[1] // 2
    groups = w_s.shape[0]
    per_seq = seq // tm
    return pl.pallas_call(
        functools.partial(_sgu_out_kernel, groups=groups, gw=half // groups),
        out_shape=jax.ShapeDtypeStruct((t, d), F32),
        grid=(t // tm,),
        in_specs=[
            pl.BlockSpec((tm, half), lambda i: (i, 0)),
            pl.BlockSpec((tm, half), lambda i: (i, 1)),
            pl.BlockSpec((groups, SG_CHUNK, SG_CHUNK), lambda i: (0, 0, 0)),
            pl.BlockSpec((groups, SG_CHUNK, 1), lambda i: (0, 0, 0)),
            pl.BlockSpec((half, d), lambda i: (0, 0)),
            pl.BlockSpec((tm, d), lambda i: (i, 0)),
            pl.BlockSpec((1, 6, d), lambda i: (i // per_seq, 0, 0)),
        ],
        out_specs=pl.BlockSpec((tm, d), lambda i: (i, 0)),
        scratch_shapes=[pltpu.VMEM((tm, half), BF16)],
        compiler_params=_params("parallel"),
        name="sgu_out",
    )(z, z, w_s, b_s.reshape(groups, SG_CHUNK, 1), w_out, x, mod)


def _ffn_kernel(x_ref, g_ref, mod_ref, w1_ref, w3_ref, w2_ref, o_ref, h_ref, acc_ref):
    j = pl.program_id(1)

    @pl.when(j == 0)
    def _():
        h_ref[...] = _norm_mod(x_ref[...], g_ref[...], mod_ref[0, 4:5, :],
                               mod_ref[0, 3:4, :]).astype(BF16)
        acc_ref[...] = jnp.zeros_like(acc_ref)

    h = h_ref[...]
    a = jnp.dot(h, w1_ref[...], preferred_element_type=F32)
    b = jnp.dot(h, w3_ref[...], preferred_element_type=F32)
    act = (a * jax.nn.sigmoid(a) * b).astype(BF16)
    acc_ref[...] += jnp.dot(act, w2_ref[...], preferred_element_type=F32)

    @pl.when(j == pl.num_programs(1) - 1)
    def _():
        o_ref[...] = x_ref[...] + mod_ref[0, 5:6, :] * acc_ref[...]


def ffn_dense(x, g, mod, w13, w2, *, seq, tm=512, tf=1408):
    t, d = x.shape
    f = w2.shape[0]
    nf = f // tf
    per_seq = seq // tm
    return pl.pallas_call(
        _ffn_kernel,
        out_shape=jax.ShapeDtypeStruct((t, d), F32),
        grid=(t // tm, nf),
        in_specs=[
            pl.BlockSpec((tm, d), lambda i, j: (i, 0)),
            pl.BlockSpec((1, d), lambda i, j: (0, 0)),
            pl.BlockSpec((1, 6, d), lambda i, j: (i // per_seq, 0, 0)),
            pl.BlockSpec((d, tf), lambda i, j: (0, j)),
            pl.BlockSpec((d, tf), lambda i, j: (0, nf + j)),
            pl.BlockSpec((tf, d), lambda i, j: (j, 0)),
        ],
        out_specs=pl.BlockSpec((tm, d), lambda i, j: (i, 0)),
        scratch_shapes=[pltpu.VMEM((tm, d), BF16), pltpu.VMEM((tm, d), F32)],
        compiler_params=_params("parallel", "arbitrary"),
        name="ffn_dense",
    )(x, g, mod, w13, w13, w2)


def _router_kernel(x_ref, g_ref, mod_ref, wr_ref, br_ref, h_ref, sel_ref):
    h = _norm_mod(x_ref[...], g_ref[...], mod_ref[0, 4:5, :], mod_ref[0, 3:4, :])
    h_ref[...] = h
    logits = jnp.dot(h, wr_ref[...], precision=lax.Precision.HIGHEST,
                     preferred_element_type=F32) + br_ref[...]
    lane = lax.broadcasted_iota(jnp.int32, logits.shape, 1).astype(F32)
    m1 = jnp.max(logits, axis=-1, keepdims=True)
    i1 = jnp.min(jnp.where(logits == m1, lane, float(LANES)), axis=-1, keepdims=True)
    rest = jnp.where(lane == i1, NEG_INF, logits)
    m2 = jnp.max(rest, axis=-1, keepdims=True)
    i2 = jnp.min(jnp.where(rest == m2, lane, float(LANES)), axis=-1, keepdims=True)
    e = jnp.exp(m2 - m1)
    w1 = 1.0 / (1.0 + e)
    sel_ref[...] = (jnp.where(lane == 0.0, i1, 0.0) + jnp.where(lane == 1.0, i2, 0.0)
                    + jnp.where(lane == 2.0, w1, 0.0) + jnp.where(lane == 3.0, e * w1, 0.0))


def router(x, g, mod, w_r, b_r, *, seq, tm=1024):
    t, d = x.shape
    ne = w_r.shape[1]
    wr = jnp.zeros((d, LANES), F32).at[:, :ne].set(w_r)
    br = jnp.full((1, LANES), NEG_INF, F32).at[0, :ne].set(b_r)
    per_seq = seq // tm
    return pl.pallas_call(
        _router_kernel,
        out_shape=(jax.ShapeDtypeStruct((t, d), F32), jax.ShapeDtypeStruct((t, LANES), F32)),
        grid=(t // tm,),
        in_specs=[
            pl.BlockSpec((tm, d), lambda i: (i, 0)),
            pl.BlockSpec((1, d), lambda i: (0, 0)),
            pl.BlockSpec((1, 6, d), lambda i: (i // per_seq, 0, 0)),
            pl.BlockSpec((d, LANES), lambda i: (0, 0)),
            pl.BlockSpec((1, LANES), lambda i: (0, 0)),
        ],
        out_specs=(pl.BlockSpec((tm, d), lambda i: (i, 0)),
                   pl.BlockSpec((tm, LANES), lambda i: (i, 0))),
        compiler_params=_params("parallel"),
        name="router",
    )(x, g, mod, wr, br)


def dispatch_plan(sel, *, tm):
    t = sel.shape[0]
    ntiles = 2 * t // tm + N_EXPERTS
    experts = sel[:, :2].astype(jnp.int32)
    onehot = (experts[:, :, None] == jnp.arange(N_EXPERTS)[None, None, :]).astype(jnp.int32)
    per_tok = onehot.sum(axis=1)
    before = jnp.cumsum(per_tok, axis=0) - per_tok
    counts = per_tok.sum(axis=0)
    padded = (counts + tm - 1) // tm * tm
    ends = jnp.cumsum(padded)
    starts = ends - padded
    dest = jnp.take_along_axis(starts[None, :] + before, experts, axis=1)
    tokens = jnp.broadcast_to(jnp.arange(t, dtype=jnp.int32)[:, None], (t, 2))
    src = jnp.zeros((ntiles * tm,), jnp.int32).at[dest.reshape(-1)].set(tokens.reshape(-1))
    tile_start = jnp.arange(ntiles, dtype=jnp.int32) * tm
    tile_expert = jnp.minimum(jnp.searchsorted(ends, tile_start, side="right"),
                              N_EXPERTS - 1).astype(jnp.int32)
    tile_valid = (tile_start < ends[-1]).astype(jnp.int32)
    return dest, src, tile_expert, tile_valid


def _row_copy(src_hbm, row, dst_vmem, k, sem):
    return pltpu.make_async_copy(src_hbm.at[pl.ds(row, 1)], dst_vmem.at[pl.ds(k, 1)], sem)


def _gather_rows(idx_ref, base, n, src_hbm, dst_vmem, sem):
    def start(k, c):
        _row_copy(src_hbm, idx_ref[0, 0, base + k], dst_vmem, k, sem).start()
        return c

    lax.fori_loop(0, n, start, 0, unroll=8)

    def wait(k, c):
        _row_copy(src_hbm, idx_ref[0, 0, base + k], dst_vmem, k, sem).wait()
        return c

    lax.fori_loop(0, n, wait, 0, unroll=8)


def _moe_kernel(te_ref, tv_ref, src_ref, h_hbm, w1_ref, w3_ref, w2_ref, o_ref,
                xg_ref, xb_ref, acc_ref, sem):
    r = pl.program_id(0)
    j = pl.program_id(1)
    tm = xg_ref.shape[0]
    live = tv_ref[r] == 1

    @pl.when(live & (j == 0))
    def _():
        _gather_rows(src_ref, 0, tm, h_hbm, xg_ref, sem)
        xb_ref[...] = xg_ref[...].astype(BF16)
        acc_ref[...] = jnp.zeros_like(acc_ref)

    @pl.when(live)
    def _():
        xb = xb_ref[...]
        a = jnp.dot(xb, w1_ref[0], preferred_element_type=F32)
        b = jnp.dot(xb, w3_ref[0], preferred_element_type=F32)
        act = (a * jax.nn.sigmoid(a) * b).astype(BF16)
        acc_ref[...] += jnp.dot(act, w2_ref[0], preferred_element_type=F32)

    @pl.when(j == pl.num_programs(1) - 1)
    def _():
        @pl.when(live)
        def _():
            o_ref[...] = acc_ref[...]

        @pl.when(jnp.logical_not(live))
        def _():
            o_ref[...] = jnp.zeros_like(o_ref)


def moe_experts(h, src, tile_expert, tile_valid, w1, w3, w2, *, tm, tf=512):
    t, d = h.shape
    ne, _, f = w1.shape
    ntiles = tile_expert.shape[0]
    nf = f // tf

    def wcol(r, j, te, tv):
        return jnp.where(tv[r] == 1, j, nf - 1)

    return pl.pallas_call(
        _moe_kernel,
        out_shape=jax.ShapeDtypeStruct((ntiles * tm, d), F32),
        grid_spec=pltpu.PrefetchScalarGridSpec(
            num_scalar_prefetch=2,
            grid=(ntiles, nf),
            in_specs=[
                pl.BlockSpec((1, 1, tm), lambda r, j, te, tv: (r, 0, 0), memory_space=pltpu.SMEM),
                pl.BlockSpec(memory_space=pl.ANY),
                pl.BlockSpec((1, d, tf), lambda r, j, te, tv: (te[r], 0, wcol(r, j, te, tv))),
                pl.BlockSpec((1, d, tf), lambda r, j, te, tv: (te[r], 0, wcol(r, j, te, tv))),
                pl.BlockSpec((1, tf, d), lambda r, j, te, tv: (te[r], wcol(r, j, te, tv), 0)),
            ],
            out_specs=pl.BlockSpec((tm, d), lambda r, j, te, tv: (r, 0)),
            scratch_shapes=[
                pltpu.VMEM((tm, d), F32),
                pltpu.VMEM((tm, d), BF16),
                pltpu.VMEM((tm, d), F32),
                pltpu.SemaphoreType.DMA,
            ],
        ),
        compiler_params=_params("arbitrary", "arbitrary"),
        name="moe_experts",
    )(tile_expert, tile_valid, src.reshape(ntiles, 1, tm), h, w1, w3, w2)


def _combine_kernel(dest_ref, y_hbm, sel_ref, x_ref, mod_ref, o_ref, yg_ref, sem):
    tc = x_ref.shape[0]
    _gather_rows(dest_ref, 0, tc, y_hbm, yg_ref.at[0], sem)
    _gather_rows(dest_ref, tc, tc, y_hbm, yg_ref.at[1], sem)
    sel = sel_ref[...]
    lane = lax.broadcasted_iota(jnp.int32, sel.shape, 1)
    w_first = jnp.sum(jnp.where(lane == 2, sel, 0.0), axis=-1, keepdims=True)
    w_second = jnp.sum(jnp.where(lane == 3, sel, 0.0), axis=-1, keepdims=True)
    y = w_first * yg_ref[0] + w_second * yg_ref[1]
    o_ref[...] = x_ref[...] + mod_ref[0, 5:6, :] * y


def moe_combine(ys, dest, sel, x, mod, *, seq, tc=256):
    t, d = x.shape
    nblk = t // tc
    per_seq = seq // tc
    idx = dest.reshape(nblk, tc, 2).transpose(0, 2, 1).reshape(nblk, 1, 2 * tc)
    return pl.pallas_call(
        _combine_kernel,
        out_shape=jax.ShapeDtypeStruct((t, d), F32),
        grid=(nblk,),
        in_specs=[
            pl.BlockSpec((1, 1, 2 * tc), lambda i: (i, 0, 0), memory_space=pltpu.SMEM),
            pl.BlockSpec(memory_space=pl.ANY),
            pl.BlockSpec((tc, LANES), lambda i: (i, 0)),
            pl.BlockSpec((tc, d), lambda i: (i, 0)),
            pl.BlockSpec((1, 6, d), lambda i: (i // per_seq, 0, 0)),
        ],
        out_specs=pl.BlockSpec((tc, d), lambda i: (i, 0)),
        scratch_shapes=[pltpu.VMEM((2, tc, d), F32), pltpu.SemaphoreType.DMA],
        compiler_params=_params("arbitrary"),
        name="moe_combine",
    )(idx, ys, sel, x, mod)


def kernel(x, c, norm_g, ada_w, ada_b, sb_wqkv, sb_qg, sb_kg, sb_wo, sg_win, sg_vg, sg_ws, sg_bs,
           sg_wout, ca_wqkv, ca_qg, ca_kg, ca_relb, ca_wo, ff_w13, ff_w2, moe_wr, moe_br,
           moe_w1, moe_w3, moe_w2):
    batch, seq, d = x.shape
    depth = norm_g.shape[0]
    mods = ada_mods(c, ada_w, ada_b).reshape(depth, batch, 6, d)
    xt = x.reshape(batch * seq, d)
    bf = lambda w: w.astype(BF16)
    for i in range(depth):
        mod = mods[i]
        kind, j = i % 3, i // 3
        g_tok = norm_g[i, 0].reshape(1, d)
        g_ch = norm_g[i, 1].reshape(1, d)
        if kind == 0:
            qkv = norm_proj(xt, g_tok, mod, bf(sb_wqkv[j]), seq=seq, shift_idx=0, scale_idx=1)
            o = sb_attention(qkv, sb_qg[j], sb_kg[j], batch=batch, seq=seq)
            xt = out_proj_residual(o, bf(sb_wo[j]), xt, mod, seq=seq, gate_idx=2)
        elif kind == 1:
            z = sgu_in(xt, g_tok, mod, bf(sg_win[j]), sg_vg[j].reshape(1, -1), seq=seq)
            xt = sgu_out(z, sg_ws[j], sg_bs[j], bf(sg_wout[j]), xt, mod, seq=seq)
        else:
            qkv = norm_proj(xt, g_tok, mod, bf(ca_wqkv[j]), seq=seq, shift_idx=0, scale_idx=1)
            o = ca_attention(qkv, ca_qg[j], ca_kg[j], ca_relb[j], batch=batch, seq=seq)
            xt = out_proj_residual(o, bf(ca_wo[j]), xt, mod, seq=seq, gate_idx=2)
        m = i // 2
        if i % 2 == 0:
            xt = ffn_dense(xt, g_ch, mod, bf(ff_w13[m]), bf(ff_w2[m]), seq=seq)
        else:
            h, sel = router(xt, g_ch, mod, moe_wr[m], moe_br[m], seq=seq)
            dest, src, tile_expert, tile_valid = dispatch_plan(sel, tm=MOE_TM)
            ys = moe_experts(h, src, tile_expert, tile_valid, bf(moe_w1[m]), bf(moe_w3[m]),
                             bf(moe_w2[m]), tm=MOE_TM)
            xt = moe_combine(ys, dest, sel, xt, mod, seq=seq)
    return xt.reshape(batch, seq, d)
```

```python
import functools

import jax
import jax.numpy as jnp
from jax import lax
from jax.experimental import pallas as pl
from jax.experimental.pallas import tpu as pltpu

F32 = jnp.float32
BF16 = jnp.bfloat16

EPS = 1e-6
HEAD_DIM = 64
LANES = 128
CHUNK = 64
SG_CHUNK = 128
CA_PREV = 8
REL_CLIP = 128
N_EXPERTS = 8
MOE_TM = 1024
LOG2_E = 1.4426950408889634
NEG_INF = float("-inf")

VMEM_LIMIT_BYTES = 56 * 1024 * 1024


def _params(*semantics):
    return pltpu.CompilerParams(dimension_semantics=semantics, vmem_limit_bytes=VMEM_LIMIT_BYTES)


def _norm_mod(x, g, scale, shift):
    ms = jnp.mean(x * x, axis=-1, keepdims=True)
    y = x * lax.rsqrt(ms + EPS) * g
    return y * (1.0 + scale) + shift


def _head_rms_norm(x, g2):
    lane = lax.broadcasted_iota(jnp.int32, x.shape, 1)
    first = lane < HEAD_DIM
    x2 = x * x
    s0 = jnp.sum(jnp.where(first, x2, 0.0), axis=-1, keepdims=True)
    s1 = jnp.sum(jnp.where(first, 0.0, x2), axis=-1, keepdims=True)
    ms = jnp.where(first, s0, s1) * (1.0 / HEAD_DIM)
    return x * lax.rsqrt(ms + EPS) * g2


def _ada_kernel(c_ref, w_ref, b_ref, o_ref):
    c = c_ref[...]
    cond = c * jax.nn.sigmoid(c)
    o_ref[0] = jnp.dot(cond, w_ref[0], precision=lax.Precision.HIGHEST,
                       preferred_element_type=F32) + b_ref[0]


def ada_mods(c, ada_w, ada_b, *, tn=1536):
    depth, d, n = ada_w.shape
    b = c.shape[0]
    return pl.pallas_call(
        _ada_kernel,
        out_shape=jax.ShapeDtypeStruct((depth, b, n), F32),
        grid=(depth, n // tn),
        in_specs=[
            pl.BlockSpec((b, d), lambda l, j: (0, 0)),
            pl.BlockSpec((1, d, tn), lambda l, j: (l, 0, j)),
            pl.BlockSpec((1, 1, tn), lambda l, j: (l, 0, j)),
        ],
        out_specs=pl.BlockSpec((1, b, tn), lambda l, j: (l, 0, j)),
        compiler_params=_params("arbitrary", "arbitrary"),
        name="ada_mods",
    )(c, ada_w, ada_b.reshape(depth, 1, n))


def _proj_kernel(x_ref, g_ref, mod_ref, w_ref, o_ref, *, shift_idx, scale_idx, nc):
    h = _norm_mod(x_ref[...], g_ref[...], mod_ref[0, scale_idx:scale_idx + 1, :],
                  mod_ref[0, shift_idx:shift_idx + 1, :]).astype(BF16)
    n = o_ref.shape[1]
    for c0 in range(0, n, nc):
        o_ref[:, c0:c0 + nc] = jnp.dot(h, w_ref[:, c0:c0 + nc],
                                       preferred_element_type=F32).astype(BF16)


def norm_proj(x, g, mod, w, *, seq, shift_idx, scale_idx, tm=512):
    t, d = x.shape
    n = w.shape[1]
    per_seq = seq // tm
    return pl.pallas_call(
        functools.partial(_proj_kernel, shift_idx=shift_idx, scale_idx=scale_idx, nc=1024),
        out_shape=jax.ShapeDtypeStruct((t, n), BF16),
        grid=(t // tm,),
        in_specs=[
            pl.BlockSpec((tm, d), lambda i: (i, 0)),
            pl.BlockSpec((1, d), lambda i: (0, 0)),
            pl.BlockSpec((1, 6, d), lambda i: (i // per_seq, 0, 0)),
            pl.BlockSpec((d, n), lambda i: (0, 0)),
        ],
        out_specs=pl.BlockSpec((tm, n), lambda i: (i, 0)),
        compiler_params=_params("parallel"),
        name="norm_proj",
    )(x, g, mod, w)


def _out_proj_kernel(o_ref, w_ref, x_ref, mod_ref, y_ref, *, gate_idx):
    y = jnp.dot(o_ref[...], w_ref[...], preferred_element_type=F32)
    y_ref[...] = x_ref[...] + mod_ref[0, gate_idx:gate_idx + 1, :] * y


def out_proj_residual(o, w, x, mod, *, seq, gate_idx, tm=1024):
    t, d = x.shape
    k = o.shape[1]
    per_seq = seq // tm
    return pl.pallas_call(
        functools.partial(_out_proj_kernel, gate_idx=gate_idx),
        out_shape=jax.ShapeDtypeStruct((t, d), F32),
        grid=(t // tm,),
        in_specs=[
            pl.BlockSpec((tm, k), lambda i: (i, 0)),
            pl.BlockSpec((k, d), lambda i: (0, 0)),
            pl.BlockSpec((tm, d), lambda i: (i, 0)),
            pl.BlockSpec((1, 6, d), lambda i: (i // per_seq, 0, 0)),
        ],
        out_specs=pl.BlockSpec((tm, d), lambda i: (i, 0)),
        compiler_params=_params("parallel"),
        name="out_proj_residual",
    )(o, w, x, mod)


def _stack_heads(q, qs_ref):
    tq = q.shape[0]
    first = lax.broadcasted_iota(jnp.int32, q.shape, 1) < HEAD_DIM
    qs_ref[0:tq, :] = jnp.where(first, q, 0.0).astype(BF16)
    qs_ref[tq:2 * tq, :] = jnp.where(first, 0.0, q).astype(BF16)


def _unstack_heads(res, tq):
    first = lax.broadcasted_iota(jnp.int32, (tq, LANES), 1) < HEAD_DIM
    return jnp.where(first, res[0:tq, :], res[tq:2 * tq, :])


def _sb_kernel(q_ref, k_ref, v_ref, qg_ref, kg_ref, tri_ref, o_ref,
               kn_ref, qs_ref, acc_ref, carry_ref, *, tq, tk, scale):
    i = pl.program_id(2)
    gp = qs_ref.shape[0]

    @pl.when(i == 0)
    def _():
        for g in range(gp):
            cols = slice(g * LANES, (g + 1) * LANES)
            kn_ref[:, cols] = _head_rms_norm(k_ref[:, cols].astype(F32), kg_ref[...]).astype(BF16)

    for g in range(gp):
        cols = slice(g * LANES, (g + 1) * LANES)
        _stack_heads(_head_rms_norm(q_ref[:, cols].astype(F32), qg_ref[...]) * scale, qs_ref.at[g])
    acc_ref[...] = jnp.zeros_like(acc_ref)
    carry_ref[...] = jnp.zeros_like(carry_ref)

    def block(j, diag):
        k0 = pl.multiple_of(j * tk, tk)
        if diag:
            row = lax.broadcasted_iota(jnp.int32, (2 * tq, tk), 0)
            qpos = jnp.where(row >= tq, row - tq, row)
            strict = lax.broadcasted_iota(jnp.int32, (2 * tq, tk), 1) < qpos
        for g in range(gp):
            cols = slice(g * LANES, (g + 1) * LANES)
            z = lax.dot_general(qs_ref[g], kn_ref[pl.ds(k0, tk), cols], (((1,), (1,)), ((), ())),
                                preferred_element_type=F32)
            neg = jnp.minimum(z, 0.0)
            log_beta = neg - jnp.log2(1.0 + jnp.exp2(neg + neg - z))
            log_fail = log_beta - z
            if diag:
                log_fail = jnp.where(strict, log_fail, 0.0)
            after = jnp.dot(log_fail.astype(BF16), tri_ref[...], preferred_element_type=F32)
            carry = carry_ref[g]
            a = jnp.exp2(log_beta + after + carry)
            if diag:
                a = jnp.where(strict, a, 0.0)
            acc_ref[g] += jnp.dot(a.astype(BF16), v_ref[pl.ds(k0, tk), cols],
                                  preferred_element_type=F32)
            carry_ref[g] = carry + jnp.sum(log_fail, axis=-1, keepdims=True)

    block(i, True)

    def body(t, c):
        block(i - 1 - t, False)
        return c

    lax.fori_loop(0, i, body, 0)
    for g in range(gp):
        o_ref[:, g * LANES:(g + 1) * LANES] = _unstack_heads(acc_ref[g], tq).astype(BF16)


def sb_attention(qkv, q_g, k_g, *, batch, seq, tq=256, gp=4):
    t, n3 = qkv.shape
    d = n3 // 3
    groups = d // LANES // gp
    gw = gp * LANES
    tk = tq
    nq = seq // tq
    idx = jnp.arange(tk)
    tri = (idx[:, None] > idx[None, :]).astype(BF16)
    qg2 = jnp.concatenate([q_g, q_g]).reshape(1, LANES)
    kg2 = jnp.concatenate([k_g, k_g]).reshape(1, LANES)
    return pl.pallas_call(
        functools.partial(_sb_kernel, tq=tq, tk=tk, scale=HEAD_DIM ** -0.5 * LOG2_E),
        out_shape=jax.ShapeDtypeStruct((t, d), BF16),
        grid=(batch, groups, nq),
        in_specs=[
            pl.BlockSpec((tq, gw), lambda b, p, i: (b * nq + i, p)),
            pl.BlockSpec((seq, gw), lambda b, p, i: (b, groups + p)),
            pl.BlockSpec((seq, gw), lambda b, p, i: (b, 2 * groups + p)),
            pl.BlockSpec((1, LANES), lambda b, p, i: (0, 0)),
            pl.BlockSpec((1, LANES), lambda b, p, i: (0, 0)),
            pl.BlockSpec((tk, tk), lambda b, p, i: (0, 0)),
        ],
        out_specs=pl.BlockSpec((tq, gw), lambda b, p, i: (b * nq + i, p)),
        scratch_shapes=[
            pltpu.VMEM((seq, gw), BF16),
            pltpu.VMEM((gp, 2 * tq, LANES), BF16),
            pltpu.VMEM((gp, 2 * tq, LANES), F32),
            pltpu.VMEM((gp, 2 * tq, 1), F32),
        ],
        compiler_params=_params("parallel", "parallel", "arbitrary"),
        name="sb_attention",
    )(qkv, qkv, qkv, qg2, kg2, tri)


def _ca_kernel(q_ref, k_ref, v_ref, qg_ref, kg_ref, bias_ref, o_ref, kn_ref, qs_ref,
               *, tq, nblk, scale):
    i = pl.program_id(2)
    gp = qs_ref.shape[0]

    @pl.when(i == 0)
    def _():
        for g in range(gp):
            cols = slice(g * LANES, (g + 1) * LANES)
            kn_ref[:, cols] = _head_rms_norm(k_ref[:, cols].astype(F32), kg_ref[...]).astype(BF16)

    starts, live = [], []
    for o in range(nblk):
        jb = i - (nblk - 1) + o
        starts.append(pl.multiple_of(jnp.maximum(jb, 0) * tq, tq))
        live.append(jb >= 0)

    for g in range(gp):
        cols = slice(g * LANES, (g + 1) * LANES)
        _stack_heads(_head_rms_norm(q_ref[:, cols].astype(F32), qg_ref[...]) * scale, qs_ref.at[g])
        qs = qs_ref[g]
        scores = []
        for o in range(nblk):
            z = lax.dot_general(qs, kn_ref[pl.ds(starts[o], tq), cols], (((1,), (1,)), ((), ())),
                                preferred_element_type=F32)
            z = z + bias_ref[g, :, o * tq:(o + 1) * tq]
            scores.append(jnp.where(live[o], z, NEG_INF))
        m_lanes = scores[0]
        for o in range(1, nblk):
            m_lanes = jnp.maximum(m_lanes, scores[o])
        m = jnp.max(m_lanes, axis=-1, keepdims=True)
        l_lanes = jnp.zeros((2 * tq, tq), F32)
        acc = jnp.zeros((2 * tq, LANES), F32)
        for o in range(nblk):
            p = jnp.exp2(scores[o] - m)
            l_lanes = l_lanes + p
            acc = acc + jnp.dot(p.astype(BF16), v_ref[pl.ds(starts[o], tq), cols],
                                preferred_element_type=F32)
        l = jnp.sum(l_lanes, axis=-1, keepdims=True)
        o_ref[:, cols] = _unstack_heads(acc / l, tq).astype(BF16)


def ca_attention(qkv, q_g, k_g, rel_bias, *, batch, seq, tq=128, gp=8):
    t, n3 = qkv.shape
    d = n3 // 3
    pairs = d // LANES
    groups = pairs // gp
    gw = gp * LANES
    nq = seq // tq
    pad = CA_PREV * CHUNK
    nblk = pad // tq + 1
    band = nblk * tq
    heads = rel_bias.shape[0]
    span = tq + band - 1
    rel_desc = (tq - 1 + pad) - jnp.arange(span)
    row = rel_bias[:, jnp.clip(rel_desc, -REL_CLIP, REL_CLIP) + REL_CLIP].astype(F32)
    shifted = jnp.tile(row, (1, tq + 1))[:, :tq * (span + 1)].reshape(heads, tq, span + 1)
    bias = shifted[:, ::-1, :band]
    qi = jnp.arange(tq)[:, None]
    kj = jnp.arange(band)[None, :]
    qc = qi // CHUNK
    kc = (kj - pad) // CHUNK
    visible = (kc <= qc) & (kc >= qc - CA_PREV)
    bias = jnp.where(visible[None], bias * LOG2_E, NEG_INF).reshape(heads // 2, 2 * tq, band)
    qg2 = jnp.concatenate([q_g, q_g]).reshape(1, LANES)
    kg2 = jnp.concatenate([k_g, k_g]).reshape(1, LANES)
    return pl.pallas_call(
        functools.partial(_ca_kernel, tq=tq, nblk=nblk, scale=HEAD_DIM ** -0.5 * LOG2_E),
        out_shape=jax.ShapeDtypeStruct((t, d), BF16),
        grid=(batch, groups, nq),
        in_specs=[
            pl.BlockSpec((tq, gw), lambda b, p, i: (b * nq + i, p)),
            pl.BlockSpec((seq, gw), lambda b, p, i: (b, groups + p)),
            pl.BlockSpec((seq, gw), lambda b, p, i: (b, 2 * groups + p)),
            pl.BlockSpec((1, LANES), lambda b, p, i: (0, 0)),
            pl.BlockSpec((1, LANES), lambda b, p, i: (0, 0)),
            pl.BlockSpec((gp, 2 * tq, band), lambda b, p, i: (p, 0, 0)),
        ],
        out_specs=pl.BlockSpec((tq, gw), lambda b, p, i: (b * nq + i, p)),
        scratch_shapes=[pltpu.VMEM((seq, gw), BF16), pltpu.VMEM((gp, 2 * tq, LANES), BF16)],
        compiler_params=_params("parallel", "parallel", "arbitrary"),
        name="ca_attention",
    )(qkv, qkv, qkv, qg2, kg2, bias)


def _sgu_in_kernel(x_ref, g_ref, mod_ref, w_ref, vg_ref, o_ref, h_ref, z_ref, *, nc):
    j = pl.program_id(1)
    n = o_ref.shape[1]

    @pl.when(j == 0)
    def _():
        h_ref[...] = _norm_mod(x_ref[...], g_ref[...], mod_ref[0, 1:2, :],
                               mod_ref[0, 0:1, :]).astype(BF16)
        for c0 in range(0, n, nc):
            acc = jnp.dot(h_ref[...], w_ref[:, c0:c0 + nc], preferred_element_type=F32)
            o_ref[:, c0:c0 + nc] = jax.nn.gelu(acc).astype(BF16)

    @pl.when(j == 1)
    def _():
        ss = jnp.zeros((o_ref.shape[0], 1), F32)
        for c0 in range(0, n, nc):
            acc = jnp.dot(h_ref[...], w_ref[:, c0:c0 + nc], preferred_element_type=F32)
            z = jax.nn.gelu(acc)
            z_ref[:, c0:c0 + nc] = z
            ss = ss + jnp.sum(z * z, axis=-1, keepdims=True)
        rstd = lax.rsqrt(ss * (1.0 / n) + EPS)
        o_ref[...] = (z_ref[...] * rstd * vg_ref[...]).astype(BF16)


def sgu_in(x, g, mod, w_in, v_g, *, seq, tm=512):
    t, d = x.shape
    half = w_in.shape[1] // 2
    per_seq = seq // tm
    return pl.pallas_call(
        functools.partial(_sgu_in_kernel, nc=1024),
        out_shape=jax.ShapeDtypeStruct((t, 2 * half), BF16),
        grid=(t // tm, 2),
        in_specs=[
            pl.BlockSpec((tm, d), lambda i, j: (i, 0)),
            pl.BlockSpec((1, d), lambda i, j: (0, 0)),
            pl.BlockSpec((1, 6, d), lambda i, j: (i // per_seq, 0, 0)),
            pl.BlockSpec((d, half), lambda i, j: (0, j)),
            pl.BlockSpec((1, half), lambda i, j: (0, 0)),
        ],
        out_specs=pl.BlockSpec((tm, half), lambda i, j: (i, j)),
        scratch_shapes=[pltpu.VMEM((tm, d), BF16), pltpu.VMEM((tm, half), F32)],
        compiler_params=_params("parallel", "arbitrary"),
        name="sgu_in",
    )(x, g, mod, w_in, v_g)


def _sgu_out_kernel(u_ref, v_ref, ws_ref, bs_ref, wout_ref, x_ref, mod_ref, o_ref, y_ref,
                    *, groups, gw):
    tm = u_ref.shape[0]
    rows = lax.broadcasted_iota(jnp.int32, (SG_CHUNK, SG_CHUNK), 0)
    cols = lax.broadcasted_iota(jnp.int32, (SG_CHUNK, SG_CHUNK), 1)
    causal = (cols // CHUNK) <= (rows // CHUNK)
    for g in range(groups):
        wg = jnp.where(causal, ws_ref[g], 0.0).astype(BF16)
        c0 = g * gw
        for r0 in range(0, tm, SG_CHUNK):
            mixed = jnp.dot(wg, v_ref[r0:r0 + SG_CHUNK, c0:c0 + gw],
                            preferred_element_type=F32) + bs_ref[g]
            u = u_ref[r0:r0 + SG_CHUNK, c0:c0 + gw].astype(F32)
            y_ref[r0:r0 + SG_CHUNK, c0:c0 + gw] = (u * mixed).astype(BF16)
    y = jnp.dot(y_ref[...], wout_ref[...], preferred_element_type=F32)
    o_ref[...] = x_ref[...] + mod_ref[0, 2:3, :] * y


def sgu_out(z, w_s, b_s, w_out, x, mod, *, seq, tm=256):
    t, d = x.shape
    half = z.shape[1] // 2
    groups = w_s.shape[0]
    per_seq = seq // tm
    return pl.pallas_call(
        functools.partial(_sgu_out_kernel, groups=groups, gw=half // groups),
        out_shape=jax.ShapeDtypeStruct((t, d), F32),
        grid=(t // tm,),
        in_specs=[
            pl.BlockSpec((tm, half), lambda i: (i, 0)),
            pl.BlockSpec((tm, half), lambda i: (i, 1)),
            pl.BlockSpec((groups, SG_CHUNK, SG_CHUNK), lambda i: (0, 0, 0)),
            pl.BlockSpec((groups, SG_CHUNK, 1), lambda i: (0, 0, 0)),
            pl.BlockSpec((half, d), lambda i: (0, 0)),
            pl.BlockSpec((tm, d), lambda i: (i, 0)),
            pl.BlockSpec((1, 6, d), lambda i: (i // per_seq, 0, 0)),
        ],
        out_specs=pl.BlockSpec((tm, d), lambda i: (i, 0)),
        scratch_shapes=[pltpu.VMEM((tm, half), BF16)],
        compiler_params=_params("parallel"),
        name="sgu_out",
    )(z, z, w_s, b_s.reshape(groups, SG_CHUNK, 1), w_out, x, mod)


def _ffn_kernel(x_ref, g_ref, mod_ref, w1_ref, w3_ref, w2_ref, o_ref, h_ref, acc_ref):
    j = pl.program_id(1)

    @pl.when(j == 0)
    def _():
        h_ref[...] = _norm_mod(x_ref[...], g_ref[...], mod_ref[0, 4:5, :],
                               mod_ref[0, 3:4, :]).astype(BF16)
        acc_ref[...] = jnp.zeros_like(acc_ref)

    h = h_ref[...]
    a = jnp.dot(h, w1_ref[...], preferred_element_type=F32)
    b = jnp.dot(h, w3_ref[...], preferred_element_type=F32)
    act = (a * jax.nn.sigmoid(a) * b).astype(BF16)
    acc_ref[...] += jnp.dot(act, w2_ref[...], preferred_element_type=F32)

    @pl.when(j == pl.num_programs(1) - 1)
    def _():
        o_ref[...] = x_ref[...] + mod_ref[0, 5:6, :] * acc_ref[...]


def ffn_dense(x, g, mod, w13, w2, *, seq, tm=512, tf=1408):
    t, d = x.shape
    f = w2.shape[0]
    nf = f // tf
    per_seq = seq // tm
    return pl.pallas_call(
        _ffn_kernel,
        out_shape=jax.ShapeDtypeStruct((t, d), F32),
        grid=(t // tm, nf),
        in_specs=[
            pl.BlockSpec((tm, d), lambda i, j: (i, 0)),
            pl.BlockSpec((1, d), lambda i, j: (0, 0)),
            pl.BlockSpec((1, 6, d), lambda i, j: (i // per_seq, 0, 0)),
            pl.BlockSpec((d, tf), lambda i, j: (0, j)),
            pl.BlockSpec((d, tf), lambda i, j: (0, nf + j)),
            pl.BlockSpec((tf, d), lambda i, j: (j, 0)),
        ],
        out_specs=pl.BlockSpec((tm, d), lambda i, j: (i, 0)),
        scratch_shapes=[pltpu.VMEM((tm, d), BF16), pltpu.VMEM((tm, d), F32)],
        compiler_params=_params("parallel", "arbitrary"),
        name="ffn_dense",
    )(x, g, mod, w13, w13, w2)


def _router_kernel(x_ref, g_ref, mod_ref, wr_ref, br_ref, h_ref, sel_ref):
    h = _norm_mod(x_ref[...], g_ref[...], mod_ref[0, 4:5, :], mod_ref[0, 3:4, :])
    h_ref[...] = h
    logits = jnp.dot(h, wr_ref[...], precision=lax.Precision.HIGHEST,
                     preferred_element_type=F32) + br_ref[...]
    lane = lax.broadcasted_iota(jnp.int32, logits.shape, 1).astype(F32)
    m1 = jnp.max(logits, axis=-1, keepdims=True)
    i1 = jnp.min(jnp.where(logits == m1, lane, float(LANES)), axis=-1, keepdims=True)
    rest = jnp.where(lane == i1, NEG_INF, logits)
    m2 = jnp.max(rest, axis=-1, keepdims=True)
    i2 = jnp.min(jnp.where(rest == m2, lane, float(LANES)), axis=-1, keepdims=True)
    e = jnp.exp(m2 - m1)
    w1 = 1.0 / (1.0 + e)
    sel_ref[...] = (jnp.where(lane == 0.0, i1, 0.0) + jnp.where(lane == 1.0, i2, 0.0)
                    + jnp.where(lane == 2.0, w1, 0.0) + jnp.where(lane == 3.0, e * w1, 0.0))


def router(x, g, mod, w_r, b_r, *, seq, tm=1024):
    t, d = x.shape
    ne = w_r.shape[1]
    wr = jnp.zeros((d, LANES), F32).at[:, :ne].set(w_r)
    br = jnp.full((1, LANES), NEG_INF, F32).at[0, :ne].set(b_r)
    per_seq = seq // tm
    return pl.pallas_call(
        _router_kernel,
        out_shape=(jax.ShapeDtypeStruct((t, d), F32), jax.ShapeDtypeStruct((t, LANES), F32)),
        grid=(t // tm,),
        in_specs=[
            pl.BlockSpec((tm, d), lambda i: (i, 0)),
            pl.BlockSpec((1, d), lambda i: (0, 0)),
            pl.BlockSpec((1, 6, d), lambda i: (i // per_seq, 0, 0)),
            pl.BlockSpec((d, LANES), lambda i: (0, 0)),
            pl.BlockSpec((1, LANES), lambda i: (0, 0)),
        ],
        out_specs=(pl.BlockSpec((tm, d), lambda i: (i, 0)),
                   pl.BlockSpec((tm, LANES), lambda i: (i, 0))),
        compiler_params=_params("parallel"),
        name="router",
    )(x, g, mod, wr, br)


def dispatch_plan(sel, *, tm):
    t = sel.shape[0]
    ntiles = 2 * t // tm + N_EXPERTS
    experts = sel[:, :2].astype(jnp.int32)
    onehot = (experts[:, :, None] == jnp.arange(N_EXPERTS)[None, None, :]).astype(jnp.int32)
    per_tok = onehot.sum(axis=1)
    before = jnp.cumsum(per_tok, axis=0) - per_tok
    counts = per_tok.sum(axis=0)
    padded = (counts + tm - 1) // tm * tm
    ends = jnp.cumsum(padded)
    starts = ends - padded
    dest = jnp.take_along_axis(starts[None, :] + before, experts, axis=1)
    tokens = jnp.broadcast_to(jnp.arange(t, dtype=jnp.int32)[:, None], (t, 2))
    src = jnp.zeros((ntiles * tm,), jnp.int32).at[dest.reshape(-1)].set(tokens.reshape(-1))
    tile_start = jnp.arange(ntiles, dtype=jnp.int32) * tm
    tile_expert = jnp.minimum((tile_start[:, None] >= ends[None, :]).sum(axis=1),
                              N_EXPERTS - 1).astype(jnp.int32)
    tile_valid = (tile_start < ends[-1]).astype(jnp.int32)
    return dest, src, tile_expert, tile_valid


def _row_copy(src_hbm, row, dst_vmem, k, sem):
    return pltpu.make_async_copy(src_hbm.at[pl.ds(row, 1)], dst_vmem.at[pl.ds(k, 1)], sem)


def _gather_rows(idx_ref, base, n, src_hbm, dst_vmem, sem):
    def start(k, c):
        _row_copy(src_hbm, idx_ref[0, 0, base + k], dst_vmem, k, sem).start()
        return c

    lax.fori_loop(0, n, start, 0, unroll=8)

    pltpu.make_async_copy(src_hbm.at[pl.ds(0, n)], dst_vmem.at[pl.ds(0, n)], sem).wait()


def _moe_kernel(te_ref, tv_ref, src_ref, h_hbm, w1_ref, w3_ref, w2_ref, o_ref,
                xg_ref, xb_ref, acc_ref, sem):
    r = pl.program_id(0)
    j = pl.program_id(1)
    tm = xg_ref.shape[0]
    live = tv_ref[r] == 1

    @pl.when(live & (j == 0))
    def _():
        _gather_rows(src_ref, 0, tm, h_hbm, xg_ref, sem)
        xb_ref[...] = xg_ref[...].astype(BF16)
        acc_ref[...] = jnp.zeros_like(acc_ref)

    @pl.when(live)
    def _():
        xb = xb_ref[...]
        a = jnp.dot(xb, w1_ref[0], preferred_element_type=F32)
        b = jnp.dot(xb, w3_ref[0], preferred_element_type=F32)
        act = (a * jax.nn.sigmoid(a) * b).astype(BF16)
        acc_ref[...] += jnp.dot(act, w2_ref[0], preferred_element_type=F32)

    @pl.when(j == pl.num_programs(1) - 1)
    def _():
        @pl.when(live)
        def _():
            o_ref[...] = acc_ref[...]

        @pl.when(jnp.logical_not(live))
        def _():
            o_ref[...] = jnp.zeros_like(o_ref)


def moe_experts(h, src, tile_expert, tile_valid, w1, w3, w2, *, tm, tf=896):
    t, d = h.shape
    ne, _, f = w1.shape
    ntiles = tile_expert.shape[0]
    nf = f // tf

    def wcol(r, j, te, tv):
        return jnp.where(tv[r] == 1, j, nf - 1)

    return pl.pallas_call(
        _moe_kernel,
        out_shape=jax.ShapeDtypeStruct((ntiles * tm, d), F32),
        grid_spec=pltpu.PrefetchScalarGridSpec(
            num_scalar_prefetch=2,
            grid=(ntiles, nf),
            in_specs=[
                pl.BlockSpec((1, 1, tm), lambda r, j, te, tv: (r, 0, 0), memory_space=pltpu.SMEM),
                pl.BlockSpec(memory_space=pl.ANY),
                pl.BlockSpec((1, d, tf), lambda r, j, te, tv: (te[r], 0, wcol(r, j, te, tv))),
                pl.BlockSpec((1, d, tf), lambda r, j, te, tv: (te[r], 0, wcol(r, j, te, tv))),
                pl.BlockSpec((1, tf, d), lambda r, j, te, tv: (te[r], wcol(r, j, te, tv), 0)),
            ],
            out_specs=pl.BlockSpec((tm, d), lambda r, j, te, tv: (r, 0)),
            scratch_shapes=[
                pltpu.VMEM((tm, d), F32),
                pltpu.VMEM((tm, d), BF16),
                pltpu.VMEM((tm, d), F32),
                pltpu.SemaphoreType.DMA,
            ],
        ),
        compiler_params=_params("arbitrary", "arbitrary"),
        name="moe_experts",
    )(tile_expert, tile_valid, src.reshape(ntiles, 1, tm), h, w1, w3, w2)


def _combine_kernel(dest_ref, y_hbm, sel_ref, x_ref, mod_ref, o_ref, yg_ref, sem):
    tc = x_ref.shape[0]
    _gather_rows(dest_ref, 0, tc, y_hbm, yg_ref.at[0], sem)
    _gather_rows(dest_ref, tc, tc, y_hbm, yg_ref.at[1], sem)
    sel = sel_ref[...]
    lane = lax.broadcasted_iota(jnp.int32, sel.shape, 1)
    w_first = jnp.sum(jnp.where(lane == 2, sel, 0.0), axis=-1, keepdims=True)
    w_second = jnp.sum(jnp.where(lane == 3, sel, 0.0), axis=-1, keepdims=True)
    y = w_first * yg_ref[0] + w_second * yg_ref[1]
    o_ref[...] = x_ref[...] + mod_ref[0, 5:6, :] * y


def moe_combine(ys, dest, sel, x, mod, *, seq, tc=512):
    t, d = x.shape
    nblk = t // tc
    per_seq = seq // tc
    idx = dest.reshape(nblk, tc, 2).transpose(0, 2, 1).reshape(nblk, 1, 2 * tc)
    return pl.pallas_call(
        _combine_kernel,
        out_shape=jax.ShapeDtypeStruct((t, d), F32),
        grid=(nblk,),
        in_specs=[
            pl.BlockSpec((1, 1, 2 * tc), lambda i: (i, 0, 0), memory_space=pltpu.SMEM),
            pl.BlockSpec(memory_space=pl.ANY),
            pl.BlockSpec((tc, LANES), lambda i: (i, 0)),
            pl.BlockSpec((tc, d), lambda i: (i, 0)),
            pl.BlockSpec((1, 6, d), lambda i: (i // per_seq, 0, 0)),
        ],
        out_specs=pl.BlockSpec((tc, d), lambda i: (i, 0)),
        scratch_shapes=[pltpu.VMEM((2, tc, d), F32), pltpu.SemaphoreType.DMA],
        compiler_params=_params("arbitrary"),
        name="moe_combine",
    )(idx, ys, sel, x, mod)


def kernel(x, c, norm_g, ada_w, ada_b, sb_wqkv, sb_qg, sb_kg, sb_wo, sg_win, sg_vg, sg_ws, sg_bs,
           sg_wout, ca_wqkv, ca_qg, ca_kg, ca_relb, ca_wo, ff_w13, ff_w2, moe_wr, moe_br,
           moe_w1, moe_w3, moe_w2):
    batch, seq, d = x.shape
    depth = norm_g.shape[0]
    mods = ada_mods(c, ada_w, ada_b).reshape(depth, batch, 6, d)
    xt = x.reshape(batch * seq, d)
    bf = lambda w: w.astype(BF16)
    for i in range(depth):
        mod = mods[i]
        kind, j = i % 3, i // 3
        g_tok = norm_g[i, 0].reshape(1, d)
        g_ch = norm_g[i, 1].reshape(1, d)
        if kind == 0:
            qkv = norm_proj(xt, g_tok, mod, bf(sb_wqkv[j]), seq=seq, shift_idx=0, scale_idx=1)
            o = sb_attention(qkv, sb_qg[j], sb_kg[j], batch=batch, seq=seq)
            xt = out_proj_residual(o, bf(sb_wo[j]), xt, mod, seq=seq, gate_idx=2)
        elif kind == 1:
            z = sgu_in(xt, g_tok, mod, bf(sg_win[j]), sg_vg[j].reshape(1, -1), seq=seq)
            xt = sgu_out(z, sg_ws[j], sg_bs[j], bf(sg_wout[j]), xt, mod, seq=seq)
        else:
            qkv = norm_proj(xt, g_tok, mod, bf(ca_wqkv[j]), seq=seq, shift_idx=0, scale_idx=1)
            o = ca_attention(qkv, ca_qg[j], ca_kg[j], ca_relb[j], batch=batch, seq=seq)
            xt = out_proj_residual(o, bf(ca_wo[j]), xt, mod, seq=seq, gate_idx=2)
        m = i // 2
        if i % 2 == 0:
            xt = ffn_dense(xt, g_ch, mod, bf(ff_w13[m]), bf(ff_w2[m]), seq=seq)
        else:
            h, sel = router(xt, g_ch, mod, moe_wr[m], moe_br[m], seq=seq)
            dest, src, tile_expert, tile_valid = dispatch_plan(sel, tm=MOE_TM)
            ys = moe_experts(h, src, tile_expert, tile_valid, bf(moe_w1[m]), bf(moe_w3[m]),
                             bf(moe_w2[m]), tm=MOE_TM)
            xt = moe_combine(ys, dest, sel, xt, mod, seq=seq)
    return xt.reshape(batch, seq, d)
```

```python
import functools

import jax
import jax.numpy as jnp
from jax import lax
from jax.experimental import pallas as pl
from jax.experimental.pallas import tpu as pltpu

F32 = jnp.float32
BF16 = jnp.bfloat16

EPS = 1e-6
HEAD_DIM = 64
LANES = 128
CHUNK = 64
SG_CHUNK = 128
CA_PREV = 8
REL_CLIP = 128
N_EXPERTS = 8
MOE_TM = 512
LOG2_E = 1.4426950408889634
NEG_INF = float("-inf")

VMEM_LIMIT_BYTES = 56 * 1024 * 1024


def _params(*semantics):
    return pltpu.CompilerParams(dimension_semantics=semantics, vmem_limit_bytes=VMEM_LIMIT_BYTES)


def _norm_mod(x, g, scale, shift):
    ms = jnp.mean(x * x, axis=-1, keepdims=True)
    y = x * lax.rsqrt(ms + EPS) * g
    return y * (1.0 + scale) + shift


def _head_rms_norm(x, g2):
    lane = lax.broadcasted_iota(jnp.int32, x.shape, 1)
    first = lane < HEAD_DIM
    x2 = x * x
    s0 = jnp.sum(jnp.where(first, x2, 0.0), axis=-1, keepdims=True)
    s1 = jnp.sum(jnp.where(first, 0.0, x2), axis=-1, keepdims=True)
    ms = jnp.where(first, s0, s1) * (1.0 / HEAD_DIM)
    return x * lax.rsqrt(ms + EPS) * g2


def _ada_kernel(c_ref, w_ref, b_ref, o_ref):
    c = c_ref[...]
    cond = c * jax.nn.sigmoid(c)
    o_ref[0] = jnp.dot(cond, w_ref[0], precision=lax.Precision.HIGHEST,
                       preferred_element_type=F32) + b_ref[0]


def ada_mods(c, ada_w, ada_b, *, tn=1536):
    depth, d, n = ada_w.shape
    b = c.shape[0]
    return pl.pallas_call(
        _ada_kernel,
        out_shape=jax.ShapeDtypeStruct((depth, b, n), F32),
        grid=(depth, n // tn),
        in_specs=[
            pl.BlockSpec((b, d), lambda l, j: (0, 0)),
            pl.BlockSpec((1, d, tn), lambda l, j: (l, 0, j)),
            pl.BlockSpec((1, 1, tn), lambda l, j: (l, 0, j)),
        ],
        out_specs=pl.BlockSpec((1, b, tn), lambda l, j: (l, 0, j)),
        compiler_params=_params("arbitrary", "arbitrary"),
        name="ada_mods",
    )(c, ada_w, ada_b.reshape(depth, 1, n))


def _proj_kernel(x_ref, g_ref, mod_ref, w_ref, o_ref, *, shift_idx, scale_idx, nc):
    h = _norm_mod(x_ref[...], g_ref[...], mod_ref[0, scale_idx:scale_idx + 1, :],
                  mod_ref[0, shift_idx:shift_idx + 1, :]).astype(BF16)
    n = o_ref.shape[1]
    for c0 in range(0, n, nc):
        o_ref[:, c0:c0 + nc] = jnp.dot(h, w_ref[:, c0:c0 + nc],
                                       preferred_element_type=F32).astype(BF16)


def norm_proj(x, g, mod, w, *, seq, shift_idx, scale_idx, tm=512):
    t, d = x.shape
    n = w.shape[1]
    per_seq = seq // tm
    return pl.pallas_call(
        functools.partial(_proj_kernel, shift_idx=shift_idx, scale_idx=scale_idx, nc=1024),
        out_shape=jax.ShapeDtypeStruct((t, n), BF16),
        grid=(t // tm,),
        in_specs=[
            pl.BlockSpec((tm, d), lambda i: (i, 0)),
            pl.BlockSpec((1, d), lambda i: (0, 0)),
            pl.BlockSpec((1, 6, d), lambda i: (i // per_seq, 0, 0)),
            pl.BlockSpec((d, n), lambda i: (0, 0)),
        ],
        out_specs=pl.BlockSpec((tm, n), lambda i: (i, 0)),
        compiler_params=_params("parallel"),
        name="norm_proj",
    )(x, g, mod, w)


def _out_proj_kernel(o_ref, w_ref, x_ref, mod_ref, y_ref, *, gate_idx):
    y = jnp.dot(o_ref[...], w_ref[...], preferred_element_type=F32)
    y_ref[...] = x_ref[...] + mod_ref[0, gate_idx:gate_idx + 1, :] * y


def out_proj_residual(o, w, x, mod, *, seq, gate_idx, tm=1024):
    t, d = x.shape
    k = o.shape[1]
    per_seq = seq // tm
    return pl.pallas_call(
        functools.partial(_out_proj_kernel, gate_idx=gate_idx),
        out_shape=jax.ShapeDtypeStruct((t, d), F32),
        grid=(t // tm,),
        in_specs=[
            pl.BlockSpec((tm, k), lambda i: (i, 0)),
            pl.BlockSpec((k, d), lambda i: (0, 0)),
            pl.BlockSpec((tm, d), lambda i: (i, 0)),
            pl.BlockSpec((1, 6, d), lambda i: (i // per_seq, 0, 0)),
        ],
        out_specs=pl.BlockSpec((tm, d), lambda i: (i, 0)),
        compiler_params=_params("parallel"),
        name="out_proj_residual",
    )(o, w, x, mod)


def _stack_heads(q, qs_ref):
    tq = q.shape[0]
    first = lax.broadcasted_iota(jnp.int32, q.shape, 1) < HEAD_DIM
    qs_ref[0:tq, :] = jnp.where(first, q, 0.0).astype(BF16)
    qs_ref[tq:2 * tq, :] = jnp.where(first, 0.0, q).astype(BF16)


def _unstack_heads(res, tq):
    first = lax.broadcasted_iota(jnp.int32, (tq, LANES), 1) < HEAD_DIM
    return jnp.where(first, res[0:tq, :], res[tq:2 * tq, :])


def _sb_kernel(q_ref, k_ref, v_ref, qg_ref, kg_ref, tri_ref, o_ref,
               kn_ref, qs_ref, acc_ref, carry_ref, *, tq, tk, scale):
    i = pl.program_id(2)
    gp = qs_ref.shape[0]

    @pl.when(i == 0)
    def _():
        for g in range(gp):
            cols = slice(g * LANES, (g + 1) * LANES)
            kn_ref[:, cols] = _head_rms_norm(k_ref[:, cols].astype(F32), kg_ref[...]).astype(BF16)

    for g in range(gp):
        cols = slice(g * LANES, (g + 1) * LANES)
        _stack_heads(_head_rms_norm(q_ref[:, cols].astype(F32), qg_ref[...]) * scale, qs_ref.at[g])
    acc_ref[...] = jnp.zeros_like(acc_ref)
    carry_ref[...] = jnp.zeros_like(carry_ref)

    def block(j, diag):
        k0 = pl.multiple_of(j * tk, tk)
        if diag:
            row = lax.broadcasted_iota(jnp.int32, (2 * tq, tk), 0)
            qpos = jnp.where(row >= tq, row - tq, row)
            strict = lax.broadcasted_iota(jnp.int32, (2 * tq, tk), 1) < qpos
        for g in range(gp):
            cols = slice(g * LANES, (g + 1) * LANES)
            z = lax.dot_general(qs_ref[g], kn_ref[pl.ds(k0, tk), cols], (((1,), (1,)), ((), ())),
                                preferred_element_type=F32)
            neg = jnp.minimum(z, 0.0)
            log_beta = neg - jnp.log2(1.0 + jnp.exp2(neg + neg - z))
            log_fail = log_beta - z
            if diag:
                log_fail = jnp.where(strict, log_fail, 0.0)
            after = jnp.dot(log_fail.astype(BF16), tri_ref[...], preferred_element_type=F32)
            carry = carry_ref[g]
            a = jnp.exp2(log_beta + after + carry)
            if diag:
                a = jnp.where(strict, a, 0.0)
            acc_ref[g] += jnp.dot(a.astype(BF16), v_ref[pl.ds(k0, tk), cols],
                                  preferred_element_type=F32)
            carry_ref[g] = carry + jnp.sum(log_fail, axis=-1, keepdims=True)

    block(i, True)

    def body(t, c):
        block(i - 1 - t, False)
        return c

    lax.fori_loop(0, i, body, 0)
    for g in range(gp):
        o_ref[:, g * LANES:(g + 1) * LANES] = _unstack_heads(acc_ref[g], tq).astype(BF16)


def sb_attention(qkv, q_g, k_g, *, batch, seq, tq=256, gp=4):
    t, n3 = qkv.shape
    d = n3 // 3
    groups = d // LANES // gp
    gw = gp * LANES
    tk = tq
    nq = seq // tq
    idx = jnp.arange(tk)
    tri = (idx[:, None] > idx[None, :]).astype(BF16)
    qg2 = jnp.concatenate([q_g, q_g]).reshape(1, LANES)
    kg2 = jnp.concatenate([k_g, k_g]).reshape(1, LANES)
    return pl.pallas_call(
        functools.partial(_sb_kernel, tq=tq, tk=tk, scale=HEAD_DIM ** -0.5 * LOG2_E),
        out_shape=jax.ShapeDtypeStruct((t, d), BF16),
        grid=(batch, groups, nq),
        in_specs=[
            pl.BlockSpec((tq, gw), lambda b, p, i: (b * nq + i, p)),
            pl.BlockSpec((seq, gw), lambda b, p, i: (b, groups + p)),
            pl.BlockSpec((seq, gw), lambda b, p, i: (b, 2 * groups + p)),
            pl.BlockSpec((1, LANES), lambda b, p, i: (0, 0)),
            pl.BlockSpec((1, LANES), lambda b, p, i: (0, 0)),
            pl.BlockSpec((tk, tk), lambda b, p, i: (0, 0)),
        ],
        out_specs=pl.BlockSpec((tq, gw), lambda b, p, i: (b * nq + i, p)),
        scratch_shapes=[
            pltpu.VMEM((seq, gw), BF16),
            pltpu.VMEM((gp, 2 * tq, LANES), BF16),
            pltpu.VMEM((gp, 2 * tq, LANES), F32),
            pltpu.VMEM((gp, 2 * tq, 1), F32),
        ],
        compiler_params=_params("parallel", "parallel", "arbitrary"),
        name="sb_attention",
    )(qkv, qkv, qkv, qg2, kg2, tri)


def _ca_kernel(q_ref, k_ref, v_ref, qg_ref, kg_ref, bias_ref, o_ref, kn_ref, qs_ref,
               *, tq, nblk, scale):
    i = pl.program_id(2)
    gp = qs_ref.shape[0]

    @pl.when(i == 0)
    def _():
        for g in range(gp):
            cols = slice(g * LANES, (g + 1) * LANES)
            kn_ref[:, cols] = _head_rms_norm(k_ref[:, cols].astype(F32), kg_ref[...]).astype(BF16)

    starts, live = [], []
    for o in range(nblk):
        jb = i - (nblk - 1) + o
        starts.append(pl.multiple_of(jnp.maximum(jb, 0) * tq, tq))
        live.append(jb >= 0)

    for g in range(gp):
        cols = slice(g * LANES, (g + 1) * LANES)
        _stack_heads(_head_rms_norm(q_ref[:, cols].astype(F32), qg_ref[...]) * scale, qs_ref.at[g])
        qs = qs_ref[g]
        scores = []
        for o in range(nblk):
            z = lax.dot_general(qs, kn_ref[pl.ds(starts[o], tq), cols], (((1,), (1,)), ((), ())),
                                preferred_element_type=F32)
            z = z + bias_ref[g, :, o * tq:(o + 1) * tq]
            scores.append(jnp.where(live[o], z, NEG_INF))
        m_lanes = scores[0]
        for o in range(1, nblk):
            m_lanes = jnp.maximum(m_lanes, scores[o])
        m = jnp.max(m_lanes, axis=-1, keepdims=True)
        l_lanes = jnp.zeros((2 * tq, tq), F32)
        acc = jnp.zeros((2 * tq, LANES), F32)
        for o in range(nblk):
            p = jnp.exp2(scores[o] - m)
            l_lanes = l_lanes + p
            acc = acc + jnp.dot(p.astype(BF16), v_ref[pl.ds(starts[o], tq), cols],
                                preferred_element_type=F32)
        l = jnp.sum(l_lanes, axis=-1, keepdims=True)
        o_ref[:, cols] = _unstack_heads(acc / l, tq).astype(BF16)


def ca_attention(qkv, q_g, k_g, rel_bias, *, batch, seq, tq=128, gp=8):
    t, n3 = qkv.shape
    d = n3 // 3
    pairs = d // LANES
    groups = pairs // gp
    gw = gp * LANES
    nq = seq // tq
    pad = CA_PREV * CHUNK
    nblk = pad // tq + 1
    band = nblk * tq
    heads = rel_bias.shape[0]
    span = tq + band - 1
    rel_of_m = pad - ((jnp.arange(span) + tq - 1) % span - (tq - 1))
    row = rel_bias[:, jnp.clip(rel_of_m, -REL_CLIP, REL_CLIP) + REL_CLIP].astype(F32)
    bias = jnp.tile(row, (1, tq))[:, :tq * (span - 1)].reshape(heads, tq, span - 1)[:, :, :band]
    qi = jnp.arange(tq)[:, None]
    kj = jnp.arange(band)[None, :]
    qc = qi // CHUNK
    kc = (kj - pad) // CHUNK
    visible = (kc <= qc) & (kc >= qc - CA_PREV)
    bias = jnp.where(visible[None], bias * LOG2_E, NEG_INF).reshape(heads // 2, 2 * tq, band)
    qg2 = jnp.concatenate([q_g, q_g]).reshape(1, LANES)
    kg2 = jnp.concatenate([k_g, k_g]).reshape(1, LANES)
    return pl.pallas_call(
        functools.partial(_ca_kernel, tq=tq, nblk=nblk, scale=HEAD_DIM ** -0.5 * LOG2_E),
        out_shape=jax.ShapeDtypeStruct((t, d), BF16),
        grid=(batch, groups, nq),
        in_specs=[
            pl.BlockSpec((tq, gw), lambda b, p, i: (b * nq + i, p)),
            pl.BlockSpec((seq, gw), lambda b, p, i: (b, groups + p)),
            pl.BlockSpec((seq, gw), lambda b, p, i: (b, 2 * groups + p)),
            pl.BlockSpec((1, LANES), lambda b, p, i: (0, 0)),
            pl.BlockSpec((1, LANES), lambda b, p, i: (0, 0)),
            pl.BlockSpec((gp, 2 * tq, band), lambda b, p, i: (p, 0, 0)),
        ],
        out_specs=pl.BlockSpec((tq, gw), lambda b, p, i: (b * nq + i, p)),
        scratch_shapes=[pltpu.VMEM((seq, gw), BF16), pltpu.VMEM((gp, 2 * tq, LANES), BF16)],
        compiler_params=_params("parallel", "parallel", "arbitrary"),
        name="ca_attention",
    )(qkv, qkv, qkv, qg2, kg2, bias)


def _sgu_in_kernel(x_ref, g_ref, mod_ref, w_ref, vg_ref, o_ref, h_ref, z_ref, *, nc):
    j = pl.program_id(0)
    n = o_ref.shape[1]
    h_ref[...] = _norm_mod(x_ref[...], g_ref[...], mod_ref[0, 1:2, :],
                           mod_ref[0, 0:1, :]).astype(BF16)

    @pl.when(j == 0)
    def _():
        for c0 in range(0, n, nc):
            acc = jnp.dot(h_ref[...], w_ref[:, c0:c0 + nc], preferred_element_type=F32)
            o_ref[:, c0:c0 + nc] = jax.nn.gelu(acc).astype(BF16)

    @pl.when(j == 1)
    def _():
        ss = jnp.zeros((o_ref.shape[0], 1), F32)
        for c0 in range(0, n, nc):
            acc = jnp.dot(h_ref[...], w_ref[:, c0:c0 + nc], preferred_element_type=F32)
            z = jax.nn.gelu(acc)
            z_ref[:, c0:c0 + nc] = z
            ss = ss + jnp.sum(z * z, axis=-1, keepdims=True)
        rstd = lax.rsqrt(ss * (1.0 / n) + EPS)
        o_ref[...] = (z_ref[...] * rstd * vg_ref[...]).astype(BF16)


def sgu_in(x, g, mod, w_in, v_g, *, seq, tm=512):
    t, d = x.shape
    half = w_in.shape[1] // 2
    per_seq = seq // tm
    return pl.pallas_call(
        functools.partial(_sgu_in_kernel, nc=1024),
        out_shape=jax.ShapeDtypeStruct((t, 2 * half), BF16),
        grid=(2, t // tm),
        in_specs=[
            pl.BlockSpec((tm, d), lambda j, i: (i, 0)),
            pl.BlockSpec((1, d), lambda j, i: (0, 0)),
            pl.BlockSpec((1, 6, d), lambda j, i: (i // per_seq, 0, 0)),
            pl.BlockSpec((d, half), lambda j, i: (0, j)),
            pl.BlockSpec((1, half), lambda j, i: (0, 0)),
        ],
        out_specs=pl.BlockSpec((tm, half), lambda j, i: (i, j)),
        scratch_shapes=[pltpu.VMEM((tm, d), BF16), pltpu.VMEM((tm, half), F32)],
        compiler_params=_params("arbitrary", "arbitrary"),
        name="sgu_in",
    )(x, g, mod, w_in, v_g)


def _sgu_out_kernel(u_ref, v_ref, ws_ref, bs_ref, wout_ref, x_ref, mod_ref, o_ref, y_ref,
                    *, groups, gw):
    tm = u_ref.shape[0]
    rows = lax.broadcasted_iota(jnp.int32, (SG_CHUNK, SG_CHUNK), 0)
    cols = lax.broadcasted_iota(jnp.int32, (SG_CHUNK, SG_CHUNK), 1)
    causal = (cols // CHUNK) <= (rows // CHUNK)
    for g in range(groups):
        wg = jnp.where(causal, ws_ref[g], 0.0).astype(BF16)
        c0 = g * gw
        for r0 in range(0, tm, SG_CHUNK):
            mixed = jnp.dot(wg, v_ref[r0:r0 + SG_CHUNK, c0:c0 + gw],
                            preferred_element_type=F32) + bs_ref[g]
            u = u_ref[r0:r0 + SG_CHUNK, c0:c0 + gw].astype(F32)
            y_ref[r0:r0 + SG_CHUNK, c0:c0 + gw] = (u * mixed).astype(BF16)
    y = jnp.dot(y_ref[...], wout_ref[...], preferred_element_type=F32)
    o_ref[...] = x_ref[...] + mod_ref[0, 2:3, :] * y


def sgu_out(z, w_s, b_s, w_out, x, mod, *, seq, tm=256):
    t, d = x.shape
    half = z.shape[1] // 2
    groups = w_s.shape[0]
    per_seq = seq // tm
    return pl.pallas_call(
        functools.partial(_sgu_out_kernel, groups=groups, gw=half // groups),
        out_shape=jax.ShapeDtypeStruct((t, d), F32),
        grid=(t // tm,),
        in_specs=[
            pl.BlockSpec((tm, half), lambda i: (i, 0)),
            pl.BlockSpec((tm, half), lambda i: (i, 1)),
            pl.BlockSpec((groups, SG_CHUNK, SG_CHUNK), lambda i: (0, 0, 0)),
            pl.BlockSpec((groups, SG_CHUNK, 1), lambda i: (0, 0, 0)),
            pl.BlockSpec((half, d), lambda i: (0, 0)),
            pl.BlockSpec((tm, d), lambda i: (i, 0)),
            pl.BlockSpec((1, 6, d), lambda i: (i // per_seq, 0, 0)),
        ],
        out_specs=pl.BlockSpec((tm, d), lambda i: (i, 0)),
        scratch_shapes=[pltpu.VMEM((tm, half), BF16)],
        compiler_params=_params("parallel"),
        name="sgu_out",
    )(z, z, w_s, b_s.reshape(groups, SG_CHUNK, 1), w_out, x, mod)


def _ffn_kernel(x_ref, g_ref, mod_ref, w1_ref, w3_ref, w2_ref, o_ref, h_ref, acc_ref):
    j = pl.program_id(1)

    @pl.when(j == 0)
    def _():
        h_ref[...] = _norm_mod(x_ref[...], g_ref[...], mod_ref[0, 4:5, :],
                               mod_ref[0, 3:4, :]).astype(BF16)
        acc_ref[...] = jnp.zeros_like(acc_ref)

    h = h_ref[...]
    a = jnp.dot(h, w1_ref[...], preferred_element_type=F32)
    b = jnp.dot(h, w3_ref[...], preferred_element_type=F32)
    act = (a * jax.nn.sigmoid(a) * b).astype(BF16)
    acc_ref[...] += jnp.dot(act, w2_ref[...], preferred_element_type=F32)

    @pl.when(j == pl.num_programs(1) - 1)
    def _():
        o_ref[...] = x_ref[...] + mod_ref[0, 5:6, :] * acc_ref[...]


def ffn_dense(x, g, mod, w13, w2, *, seq, tm=512):
    t, d = x.shape
    f = w2.shape[0]
    tf, nf = f, 1
    per_seq = seq // tm
    once = pl.Buffered(1)
    return pl.pallas_call(
        _ffn_kernel,
        out_shape=jax.ShapeDtypeStruct((t, d), F32),
        grid=(t // tm, nf),
        in_specs=[
            pl.BlockSpec((tm, d), lambda i, j: (i, 0)),
            pl.BlockSpec((1, d), lambda i, j: (0, 0)),
            pl.BlockSpec((1, 6, d), lambda i, j: (i // per_seq, 0, 0)),
            pl.BlockSpec((d, tf), lambda i, j: (0, j), pipeline_mode=once),
            pl.BlockSpec((d, tf), lambda i, j: (0, nf + j), pipeline_mode=once),
            pl.BlockSpec((tf, d), lambda i, j: (j, 0), pipeline_mode=once),
        ],
        out_specs=pl.BlockSpec((tm, d), lambda i, j: (i, 0)),
        scratch_shapes=[pltpu.VMEM((tm, d), BF16), pltpu.VMEM((tm, d), F32)],
        compiler_params=_params("parallel", "arbitrary"),
        name="ffn_dense",
    )(x, g, mod, w13, w13, w2)


def _router_kernel(x_ref, g_ref, mod_ref, wr_ref, br_ref, h_ref, sel_ref):
    h = _norm_mod(x_ref[...], g_ref[...], mod_ref[0, 4:5, :], mod_ref[0, 3:4, :])
    h_ref[...] = h
    logits = jnp.dot(h, wr_ref[...], precision=lax.Precision.HIGHEST,
                     preferred_element_type=F32) + br_ref[...]
    lane = lax.broadcasted_iota(jnp.int32, logits.shape, 1).astype(F32)
    m1 = jnp.max(logits, axis=-1, keepdims=True)
    i1 = jnp.min(jnp.where(logits == m1, lane, float(LANES)), axis=-1, keepdims=True)
    rest = jnp.where(lane == i1, NEG_INF, logits)
    m2 = jnp.max(rest, axis=-1, keepdims=True)
    i2 = jnp.min(jnp.where(rest == m2, lane, float(LANES)), axis=-1, keepdims=True)
    e = jnp.exp(m2 - m1)
    w1 = 1.0 / (1.0 + e)
    sel_ref[...] = (jnp.where(lane == 0.0, i1, 0.0) + jnp.where(lane == 1.0, i2, 0.0)
                    + jnp.where(lane == 2.0, w1, 0.0) + jnp.where(lane == 3.0, e * w1, 0.0))


def router(x, g, mod, w_r, b_r, *, seq, tm=1024):
    t, d = x.shape
    ne = w_r.shape[1]
    wr = jnp.zeros((d, LANES), F32).at[:, :ne].set(w_r)
    br = jnp.full((1, LANES), NEG_INF, F32).at[0, :ne].set(b_r)
    per_seq = seq // tm
    return pl.pallas_call(
        _router_kernel,
        out_shape=(jax.ShapeDtypeStruct((t, d), F32), jax.ShapeDtypeStruct((t, LANES), F32)),
        grid=(t // tm,),
        in_specs=[
            pl.BlockSpec((tm, d), lambda i: (i, 0)),
            pl.BlockSpec((1, d), lambda i: (0, 0)),
            pl.BlockSpec((1, 6, d), lambda i: (i // per_seq, 0, 0)),
            pl.BlockSpec((d, LANES), lambda i: (0, 0)),
            pl.BlockSpec((1, LANES), lambda i: (0, 0)),
        ],
        out_specs=(pl.BlockSpec((tm, d), lambda i: (i, 0)),
                   pl.BlockSpec((tm, LANES), lambda i: (i, 0))),
        compiler_params=_params("parallel"),
        name="router",
    )(x, g, mod, wr, br)


def dispatch_plan(sel, *, tm):
    t = sel.shape[0]
    ntiles = 2 * t // tm + N_EXPERTS
    experts = sel[:, :2].astype(jnp.int32)
    onehot = (experts[:, :, None] == jnp.arange(N_EXPERTS)[None, None, :]).astype(jnp.int32)
    per_tok = onehot.sum(axis=1)
    before = jnp.cumsum(per_tok, axis=0) - per_tok
    counts = per_tok.sum(axis=0)
    padded = (counts + tm - 1) // tm * tm
    ends = jnp.cumsum(padded)
    starts = ends - padded
    dest = jnp.take_along_axis(starts[None, :] + before, experts, axis=1)
    tokens = jnp.broadcast_to(jnp.arange(t, dtype=jnp.int32)[:, None], (t, 2))
    src = jnp.zeros((ntiles * tm,), jnp.int32).at[dest.reshape(-1)].set(
        tokens.reshape(-1), unique_indices=True, mode="promise_in_bounds")
    tile_start = jnp.arange(ntiles, dtype=jnp.int32) * tm
    tile_expert = jnp.minimum((tile_start[:, None] >= ends[None, :]).sum(axis=1),
                              N_EXPERTS - 1).astype(jnp.int32)
    tile_valid = (tile_start < ends[-1]).astype(jnp.int32)
    return dest, src, tile_expert, tile_valid


def _row_copy(src_hbm, row, dst_vmem, k, sem):
    return pltpu.make_async_copy(src_hbm.at[pl.ds(row, 1)], dst_vmem.at[pl.ds(k, 1)], sem)


def _gather_rows(idx_ref, base, n, src_hbm, dst_vmem, sem):
    for k in range(n):
        _row_copy(src_hbm, idx_ref[0, 0, base + k], dst_vmem, k, sem).start()

    pltpu.make_async_copy(src_hbm.at[pl.ds(0, n)], dst_vmem.at[pl.ds(0, n)], sem).wait()


def _moe_kernel(te_ref, tv_ref, src_ref, h_hbm, w1_ref, w3_ref, w2_ref, o_ref,
                xg_ref, xb_ref, acc_ref, sem):
    r = pl.program_id(0)
    j = pl.program_id(1)
    tm = xg_ref.shape[0]
    live = tv_ref[r] == 1

    @pl.when(live & (j == 0))
    def _():
        _gather_rows(src_ref, 0, tm, h_hbm, xg_ref, sem)
        xb_ref[...] = xg_ref[...].astype(BF16)
        acc_ref[...] = jnp.zeros_like(acc_ref)

    @pl.when(live)
    def _():
        xb = xb_ref[...]
        a = jnp.dot(xb, w1_ref[0, 0], preferred_element_type=F32)
        b = jnp.dot(xb, w3_ref[0, 0], preferred_element_type=F32)
        act = (a * jax.nn.sigmoid(a) * b).astype(BF16)
        acc_ref[...] += jnp.dot(act, w2_ref[0, 0], preferred_element_type=F32)

    @pl.when(j == pl.num_programs(1) - 1)
    def _():
        @pl.when(live)
        def _():
            o_ref[...] = acc_ref[...]

        @pl.when(jnp.logical_not(live))
        def _():
            o_ref[...] = jnp.zeros_like(o_ref)


def moe_experts(h, src, tile_expert, tile_valid, w1, w3, w2, *, layer, tm, tf=1792):
    t, d = h.shape
    f = w1.shape[-1]
    ntiles = tile_expert.shape[0]
    nf = f // tf

    def wcol(r, j, te, tv):
        return jnp.where(tv[r] == 1, j, nf - 1)

    return pl.pallas_call(
        _moe_kernel,
        out_shape=jax.ShapeDtypeStruct((ntiles * tm, d), F32),
        grid_spec=pltpu.PrefetchScalarGridSpec(
            num_scalar_prefetch=2,
            grid=(ntiles, nf),
            in_specs=[
                pl.BlockSpec((1, 1, tm), lambda r, j, te, tv: (r, 0, 0), memory_space=pltpu.SMEM),
                pl.BlockSpec(memory_space=pl.ANY),
                pl.BlockSpec((1, 1, d, tf),
                             lambda r, j, te, tv: (layer, te[r], 0, wcol(r, j, te, tv))),
                pl.BlockSpec((1, 1, d, tf),
                             lambda r, j, te, tv: (layer, te[r], 0, wcol(r, j, te, tv))),
                pl.BlockSpec((1, 1, tf, d),
                             lambda r, j, te, tv: (layer, te[r], wcol(r, j, te, tv), 0)),
            ],
            out_specs=pl.BlockSpec((tm, d), lambda r, j, te, tv: (r, 0)),
            scratch_shapes=[
                pltpu.VMEM((tm, d), F32),
                pltpu.VMEM((tm, d), BF16),
                pltpu.VMEM((tm, d), F32),
                pltpu.SemaphoreType.DMA,
            ],
        ),
        compiler_params=_params("arbitrary", "arbitrary"),
        name="moe_experts",
    )(tile_expert, tile_valid, src.reshape(ntiles, 1, tm), h, w1, w3, w2)


def _combine_kernel(dest_ref, y_hbm, sel_ref, x_ref, mod_ref, o_ref, yg_ref, sem):
    tc = x_ref.shape[0]
    _gather_rows(dest_ref, 0, tc, y_hbm, yg_ref.at[0], sem)
    _gather_rows(dest_ref, tc, tc, y_hbm, yg_ref.at[1], sem)
    sel = sel_ref[...]
    lane = lax.broadcasted_iota(jnp.int32, sel.shape, 1)
    w_first = jnp.sum(jnp.where(lane == 2, sel, 0.0), axis=-1, keepdims=True)
    w_second = jnp.sum(jnp.where(lane == 3, sel, 0.0), axis=-1, keepdims=True)
    y = w_first * yg_ref[0] + w_second * yg_ref[1]
    o_ref[...] = x_ref[...] + mod_ref[0, 5:6, :] * y


def moe_combine(ys, dest, sel, x, mod, *, seq, tc=512):
    t, d = x.shape
    nblk = t // tc
    per_seq = seq // tc
    idx = dest.reshape(nblk, tc, 2).transpose(0, 2, 1).reshape(nblk, 1, 2 * tc)
    return pl.pallas_call(
        _combine_kernel,
        out_shape=jax.ShapeDtypeStruct((t, d), F32),
        grid=(nblk,),
        in_specs=[
            pl.BlockSpec((1, 1, 2 * tc), lambda i: (i, 0, 0), memory_space=pltpu.SMEM),
            pl.BlockSpec(memory_space=pl.ANY),
            pl.BlockSpec((tc, LANES), lambda i: (i, 0)),
            pl.BlockSpec((tc, d), lambda i: (i, 0)),
            pl.BlockSpec((1, 6, d), lambda i: (i // per_seq, 0, 0)),
        ],
        out_specs=pl.BlockSpec((tc, d), lambda i: (i, 0)),
        scratch_shapes=[pltpu.VMEM((2, tc, d), F32), pltpu.SemaphoreType.DMA],
        compiler_params=_params("arbitrary"),
        name="moe_combine",
    )(idx, ys, sel, x, mod)


def kernel(x, c, norm_g, ada_w, ada_b, sb_wqkv, sb_qg, sb_kg, sb_wo, sg_win, sg_vg, sg_ws, sg_bs,
           sg_wout, ca_wqkv, ca_qg, ca_kg, ca_relb, ca_wo, ff_w13, ff_w2, moe_wr, moe_br,
           moe_w1, moe_w3, moe_w2):
    batch, seq, d = x.shape
    depth = norm_g.shape[0]
    mods = ada_mods(c, ada_w, ada_b).reshape(depth, batch, 6, d)
    xt = x.reshape(batch * seq, d)
    bf = lambda w: w.astype(BF16)
    moe_w1b, moe_w3b, moe_w2b = bf(moe_w1), bf(moe_w3), bf(moe_w2)
    for i in range(depth):
        mod = mods[i]
        kind, j = i % 3, i // 3
        g_tok = norm_g[i, 0].reshape(1, d)
        g_ch = norm_g[i, 1].reshape(1, d)
        if kind == 0:
            qkv = norm_proj(xt, g_tok, mod, bf(sb_wqkv[j]), seq=seq, shift_idx=0, scale_idx=1)
            o = sb_attention(qkv, sb_qg[j], sb_kg[j], batch=batch, seq=seq)
            xt = out_proj_residual(o, bf(sb_wo[j]), xt, mod, seq=seq, gate_idx=2)
        elif kind == 1:
            z = sgu_in(xt, g_tok, mod, bf(sg_win[j]), sg_vg[j].reshape(1, -1), seq=seq)
            xt = sgu_out(z, sg_ws[j], sg_bs[j], bf(sg_wout[j]), xt, mod, seq=seq)
        else:
            qkv = norm_proj(xt, g_tok, mod, bf(ca_wqkv[j]), seq=seq, shift_idx=0, scale_idx=1)
            o = ca_attention(qkv, ca_qg[j], ca_kg[j], ca_relb[j], batch=batch, seq=seq)
            xt = out_proj_residual(o, bf(ca_wo[j]), xt, mod, seq=seq, gate_idx=2)
        m = i // 2
        if i % 2 == 0:
            xt = ffn_dense(xt, g_ch, mod, bf(ff_w13[m]), bf(ff_w2[m]), seq=seq)
        else:
            h, sel = router(xt, g_ch, mod, moe_wr[m], moe_br[m], seq=seq)
            dest, src, tile_expert, tile_valid = dispatch_plan(sel, tm=MOE_TM)
            ys = moe_experts(h, src, tile_expert, tile_valid, moe_w1b, moe_w3b, moe_w2b,
                             layer=m, tm=MOE_TM)
            xt = moe_combine(ys, dest, sel, xt, mod, seq=seq)
    return xt.reshape(batch, seq, d)
```

```python
import functools

import jax
import jax.numpy as jnp
from jax import lax
from jax.experimental import pallas as pl
from jax.experimental.pallas import tpu as pltpu

F32 = jnp.float32
BF16 = jnp.bfloat16

EPS = 1e-6
HEAD_DIM = 64
LANES = 128
CHUNK = 64
SG_CHUNK = 128
CA_PREV = 8
REL_CLIP = 128
N_EXPERTS = 8
MOE_TM = 512
LOG2_E = 1.4426950408889634
SB_DEAD_LOG2 = -160.0
NEG_INF = float("-inf")

VMEM_LIMIT_BYTES = 56 * 1024 * 1024


def _params(*semantics):
    return pltpu.CompilerParams(dimension_semantics=semantics, vmem_limit_bytes=VMEM_LIMIT_BYTES)


def _norm_mod(x, g, scale, shift):
    ms = jnp.mean(x * x, axis=-1, keepdims=True)
    y = x * lax.rsqrt(ms + EPS) * g
    return y * (1.0 + scale) + shift


def _head_rms_norm(x, g2):
    lane = lax.broadcasted_iota(jnp.int32, x.shape, 1)
    first = lane < HEAD_DIM
    x2 = x * x
    s0 = jnp.sum(jnp.where(first, x2, 0.0), axis=-1, keepdims=True)
    s1 = jnp.sum(jnp.where(first, 0.0, x2), axis=-1, keepdims=True)
    ms = jnp.where(first, s0, s1) * (1.0 / HEAD_DIM)
    return x * lax.rsqrt(ms + EPS) * g2


def _ada_kernel(c_ref, w_ref, b_ref, o_ref):
    c = c_ref[...]
    cond = c * jax.nn.sigmoid(c)
    o_ref[0] = jnp.dot(cond, w_ref[0], precision=lax.Precision.HIGHEST,
                       preferred_element_type=F32) + b_ref[0]


def ada_mods(c, ada_w, ada_b, *, tn=1536):
    depth, d, n = ada_w.shape
    b = c.shape[0]
    return pl.pallas_call(
        _ada_kernel,
        out_shape=jax.ShapeDtypeStruct((depth, b, n), F32),
        grid=(depth, n // tn),
        in_specs=[
            pl.BlockSpec((b, d), lambda l, j: (0, 0)),
            pl.BlockSpec((1, d, tn), lambda l, j: (l, 0, j)),
            pl.BlockSpec((1, 1, tn), lambda l, j: (l, 0, j)),
        ],
        out_specs=pl.BlockSpec((1, b, tn), lambda l, j: (l, 0, j)),
        compiler_params=_params("arbitrary", "arbitrary"),
        name="ada_mods",
    )(c, ada_w, ada_b.reshape(depth, 1, n))


def _proj_kernel(x_ref, g_ref, mod_ref, w_ref, o_ref, *, shift_idx, scale_idx, nc):
    h = _norm_mod(x_ref[...], g_ref[...], mod_ref[0, scale_idx:scale_idx + 1, :],
                  mod_ref[0, shift_idx:shift_idx + 1, :]).astype(BF16)
    n = o_ref.shape[1]
    for c0 in range(0, n, nc):
        o_ref[:, c0:c0 + nc] = jnp.dot(h, w_ref[:, c0:c0 + nc],
                                       preferred_element_type=F32).astype(BF16)


def norm_proj(x, g, mod, w, *, seq, shift_idx, scale_idx, tm=512):
    t, d = x.shape
    n = w.shape[1]
    per_seq = seq // tm
    return pl.pallas_call(
        functools.partial(_proj_kernel, shift_idx=shift_idx, scale_idx=scale_idx, nc=1024),
        out_shape=jax.ShapeDtypeStruct((t, n), BF16),
        grid=(t // tm,),
        in_specs=[
            pl.BlockSpec((tm, d), lambda i: (i, 0)),
            pl.BlockSpec((1, d), lambda i: (0, 0)),
            pl.BlockSpec((1, 6, d), lambda i: (i // per_seq, 0, 0)),
            pl.BlockSpec((d, n), lambda i: (0, 0)),
        ],
        out_specs=pl.BlockSpec((tm, n), lambda i: (i, 0)),
        compiler_params=_params("parallel"),
        name="norm_proj",
    )(x, g, mod, w)


def _out_proj_kernel(o_ref, w_ref, x_ref, mod_ref, y_ref, *, gate_idx):
    y = jnp.dot(o_ref[...], w_ref[...], preferred_element_type=F32)
    y_ref[...] = x_ref[...] + mod_ref[0, gate_idx:gate_idx + 1, :] * y


def out_proj_residual(o, w, x, mod, *, seq, gate_idx, tm=1024):
    t, d = x.shape
    k = o.shape[1]
    per_seq = seq // tm
    return pl.pallas_call(
        functools.partial(_out_proj_kernel, gate_idx=gate_idx),
        out_shape=jax.ShapeDtypeStruct((t, d), F32),
        grid=(t // tm,),
        in_specs=[
            pl.BlockSpec((tm, k), lambda i: (i, 0)),
            pl.BlockSpec((k, d), lambda i: (0, 0)),
            pl.BlockSpec((tm, d), lambda i: (i, 0)),
            pl.BlockSpec((1, 6, d), lambda i: (i // per_seq, 0, 0)),
        ],
        out_specs=pl.BlockSpec((tm, d), lambda i: (i, 0)),
        compiler_params=_params("parallel"),
        name="out_proj_residual",
    )(o, w, x, mod)


def _stack_heads(q, qs_ref):
    tq = q.shape[0]
    first = lax.broadcasted_iota(jnp.int32, q.shape, 1) < HEAD_DIM
    qs_ref[0:tq, :] = jnp.where(first, q, 0.0).astype(BF16)
    qs_ref[tq:2 * tq, :] = jnp.where(first, 0.0, q).astype(BF16)


def _unstack_heads(res, tq):
    first = lax.broadcasted_iota(jnp.int32, (tq, LANES), 1) < HEAD_DIM
    return jnp.where(first, res[0:tq, :], res[tq:2 * tq, :])


def _sb_kernel(q_ref, k_ref, v_ref, qg_ref, kg_ref, tri_ref, o_ref,
               kn_ref, qs_ref, acc_ref, carry_ref, *, tq, tk, scale):
    i = pl.program_id(2)
    gp = qs_ref.shape[0]

    @pl.when(i == 0)
    def _():
        for g in range(gp):
            cols = slice(g * LANES, (g + 1) * LANES)
            kn_ref[:, cols] = _head_rms_norm(k_ref[:, cols].astype(F32), kg_ref[...]).astype(BF16)

    for g in range(gp):
        cols = slice(g * LANES, (g + 1) * LANES)
        _stack_heads(_head_rms_norm(q_ref[:, cols].astype(F32), qg_ref[...]) * scale, qs_ref.at[g])
    acc_ref[...] = jnp.zeros_like(acc_ref)
    carry_ref[...] = jnp.zeros_like(carry_ref)

    def block(j, diag):
        k0 = pl.multiple_of(j * tk, tk)
        if diag:
            row = lax.broadcasted_iota(jnp.int32, (2 * tq, tk), 0)
            qpos = jnp.where(row >= tq, row - tq, row)
            strict = lax.broadcasted_iota(jnp.int32, (2 * tq, tk), 1) < qpos
        for g in range(gp):
            cols = slice(g * LANES, (g + 1) * LANES)
            z = lax.dot_general(qs_ref[g], kn_ref[pl.ds(k0, tk), cols], (((1,), (1,)), ((), ())),
                                preferred_element_type=F32)
            neg = jnp.minimum(z, 0.0)
            log_beta = neg - jnp.log2(1.0 + jnp.exp2(neg + neg - z))
            log_fail = log_beta - z
            if diag:
                log_fail = jnp.where(strict, log_fail, 0.0)
            after = jnp.dot(log_fail.astype(BF16), tri_ref[...], preferred_element_type=F32)
            carry = carry_ref[g]
            a = jnp.exp2(log_beta + after + carry)
            if diag:
                a = jnp.where(strict, a, 0.0)
            acc_ref[g] += jnp.dot(a.astype(BF16), v_ref[pl.ds(k0, tk), cols],
                                  preferred_element_type=F32)
            carry_ref[g] = carry + jnp.sum(log_fail, axis=-1, keepdims=True)

    block(i, True)

    def more(t):
        return jnp.logical_and(t < i, jnp.max(carry_ref[...]) > SB_DEAD_LOG2)

    def body(t):
        block(i - 1 - t, False)
        return t + 1

    lax.while_loop(more, body, 0)
    for g in range(gp):
        o_ref[:, g * LANES:(g + 1) * LANES] = _unstack_heads(acc_ref[g], tq).astype(BF16)


def sb_attention(qkv, q_g, k_g, *, batch, seq, tq=256, gp=4):
    t, n3 = qkv.shape
    d = n3 // 3
    groups = d // LANES // gp
    gw = gp * LANES
    tk = tq
    nq = seq // tq
    idx = jnp.arange(tk)
    tri = (idx[:, None] > idx[None, :]).astype(BF16)
    qg2 = jnp.concatenate([q_g, q_g]).reshape(1, LANES)
    kg2 = jnp.concatenate([k_g, k_g]).reshape(1, LANES)
    return pl.pallas_call(
        functools.partial(_sb_kernel, tq=tq, tk=tk, scale=HEAD_DIM ** -0.5 * LOG2_E),
        out_shape=jax.ShapeDtypeStruct((t, d), BF16),
        grid=(batch, groups, nq),
        in_specs=[
            pl.BlockSpec((tq, gw), lambda b, p, i: (b * nq + i, p)),
            pl.BlockSpec((seq, gw), lambda b, p, i: (b, groups + p)),
            pl.BlockSpec((seq, gw), lambda b, p, i: (b, 2 * groups + p)),
            pl.BlockSpec((1, LANES), lambda b, p, i: (0, 0)),
            pl.BlockSpec((1, LANES), lambda b, p, i: (0, 0)),
            pl.BlockSpec((tk, tk), lambda b, p, i: (0, 0)),
        ],
        out_specs=pl.BlockSpec((tq, gw), lambda b, p, i: (b * nq + i, p)),
        scratch_shapes=[
            pltpu.VMEM((seq, gw), BF16),
            pltpu.VMEM((gp, 2 * tq, LANES), BF16),
            pltpu.VMEM((gp, 2 * tq, LANES), F32),
            pltpu.VMEM((gp, 2 * tq, 1), F32),
        ],
        compiler_params=_params("parallel", "parallel", "arbitrary"),
        name="sb_attention",
    )(qkv, qkv, qkv, qg2, kg2, tri)


def _ca_kernel(q_ref, k_ref, v_ref, qg_ref, kg_ref, bias_ref, o_ref, kn_ref, qs_ref,
               *, tq, nblk, scale):
    i = pl.program_id(2)
    gp = qs_ref.shape[0]

    @pl.when(i == 0)
    def _():
        for g in range(gp):
            cols = slice(g * LANES, (g + 1) * LANES)
            kn_ref[:, cols] = _head_rms_norm(k_ref[:, cols].astype(F32), kg_ref[...]).astype(BF16)

    starts, live = [], []
    for o in range(nblk):
        jb = i - (nblk - 1) + o
        starts.append(pl.multiple_of(jnp.maximum(jb, 0) * tq, tq))
        live.append(jb >= 0)

    for g in range(gp):
        cols = slice(g * LANES, (g + 1) * LANES)
        _stack_heads(_head_rms_norm(q_ref[:, cols].astype(F32), qg_ref[...]) * scale, qs_ref.at[g])
        qs = qs_ref[g]
        scores = []
        for o in range(nblk):
            z = lax.dot_general(qs, kn_ref[pl.ds(starts[o], tq), cols], (((1,), (1,)), ((), ())),
                                preferred_element_type=F32)
            z = z + bias_ref[g, :, o * tq:(o + 1) * tq]
            scores.append(jnp.where(live[o], z, NEG_INF))
        m_lanes = scores[0]
        for o in range(1, nblk):
            m_lanes = jnp.maximum(m_lanes, scores[o])
        m = jnp.max(m_lanes, axis=-1, keepdims=True)
        l_lanes = jnp.zeros((2 * tq, tq), F32)
        acc = jnp.zeros((2 * tq, LANES), F32)
        for o in range(nblk):
            p = jnp.exp2(scores[o] - m)
            l_lanes = l_lanes + p
            acc = acc + jnp.dot(p.astype(BF16), v_ref[pl.ds(starts[o], tq), cols],
                                preferred_element_type=F32)
        l = jnp.sum(l_lanes, axis=-1, keepdims=True)
        o_ref[:, cols] = _unstack_heads(acc / l, tq).astype(BF16)


def ca_attention(qkv, q_g, k_g, rel_bias, *, batch, seq, tq=128, gp=8):
    t, n3 = qkv.shape
    d = n3 // 3
    pairs = d // LANES
    groups = pairs // gp
    gw = gp * LANES
    nq = seq // tq
    pad = CA_PREV * CHUNK
    nblk = pad // tq + 1
    band = nblk * tq
    heads = rel_bias.shape[0]
    span = tq + band - 1
    rel_of_m = pad - ((jnp.arange(span) + tq - 1) % span - (tq - 1))
    row = rel_bias[:, jnp.clip(rel_of_m, -REL_CLIP, REL_CLIP) + REL_CLIP].astype(F32)
    bias = jnp.tile(row, (1, tq))[:, :tq * (span - 1)].reshape(heads, tq, span - 1)[:, :, :band]
    qi = jnp.arange(tq)[:, None]
    kj = jnp.arange(band)[None, :]
    qc = qi // CHUNK
    kc = (kj - pad) // CHUNK
    visible = (kc <= qc) & (kc >= qc - CA_PREV)
    bias = jnp.where(visible[None], bias * LOG2_E, NEG_INF).reshape(heads // 2, 2 * tq, band)
    qg2 = jnp.concatenate([q_g, q_g]).reshape(1, LANES)
    kg2 = jnp.concatenate([k_g, k_g]).reshape(1, LANES)
    return pl.pallas_call(
        functools.partial(_ca_kernel, tq=tq, nblk=nblk, scale=HEAD_DIM ** -0.5 * LOG2_E),
        out_shape=jax.ShapeDtypeStruct((t, d), BF16),
        grid=(batch, groups, nq),
        in_specs=[
            pl.BlockSpec((tq, gw), lambda b, p, i: (b * nq + i, p)),
            pl.BlockSpec((seq, gw), lambda b, p, i: (b, groups + p)),
            pl.BlockSpec((seq, gw), lambda b, p, i: (b, 2 * groups + p)),
            pl.BlockSpec((1, LANES), lambda b, p, i: (0, 0)),
            pl.BlockSpec((1, LANES), lambda b, p, i: (0, 0)),
            pl.BlockSpec((gp, 2 * tq, band), lambda b, p, i: (p, 0, 0)),
        ],
        out_specs=pl.BlockSpec((tq, gw), lambda b, p, i: (b * nq + i, p)),
        scratch_shapes=[pltpu.VMEM((seq, gw), BF16), pltpu.VMEM((gp, 2 * tq, LANES), BF16)],
        compiler_params=_params("parallel", "parallel", "arbitrary"),
        name="ca_attention",
    )(qkv, qkv, qkv, qg2, kg2, bias)


def _sgu_in_kernel(x_ref, g_ref, mod_ref, w_ref, vg_ref, o_ref, h_ref, z_ref, *, nc):
    j = pl.program_id(0)
    n = o_ref.shape[1]
    h_ref[...] = _norm_mod(x_ref[...], g_ref[...], mod_ref[0, 1:2, :],
                           mod_ref[0, 0:1, :]).astype(BF16)

    @pl.when(j == 0)
    def _():
        for c0 in range(0, n, nc):
            acc = jnp.dot(h_ref[...], w_ref[:, c0:c0 + nc], preferred_element_type=F32)
            o_ref[:, c0:c0 + nc] = jax.nn.gelu(acc).astype(BF16)

    @pl.when(j == 1)
    def _():
        ss = jnp.zeros((o_ref.shape[0], 1), F32)
        for c0 in range(0, n, nc):
            acc = jnp.dot(h_ref[...], w_ref[:, c0:c0 + nc], preferred_element_type=F32)
            z = jax.nn.gelu(acc)
            z_ref[:, c0:c0 + nc] = z
            ss = ss + jnp.sum(z * z, axis=-1, keepdims=True)
        rstd = lax.rsqrt(ss * (1.0 / n) + EPS)
        o_ref[...] = (z_ref[...] * rstd * vg_ref[...]).astype(BF16)


def sgu_in(x, g, mod, w_in, v_g, *, seq, tm=512):
    t, d = x.shape
    half = w_in.shape[1] // 2
    per_seq = seq // tm
    return pl.pallas_call(
        functools.partial(_sgu_in_kernel, nc=1024),
        out_shape=jax.ShapeDtypeStruct((t, 2 * half), BF16),
        grid=(2, t // tm),
        in_specs=[
            pl.BlockSpec((tm, d), lambda j, i: (i, 0)),
            pl.BlockSpec((1, d), lambda j, i: (0, 0)),
            pl.BlockSpec((1, 6, d), lambda j, i: (i // per_seq, 0, 0)),
            pl.BlockSpec((d, half), lambda j, i: (0, j)),
            pl.BlockSpec((1, half), lambda j, i: (0, 0)),
        ],
        out_specs=pl.BlockSpec((tm, half), lambda j, i: (i, j)),
        scratch_shapes=[pltpu.VMEM((tm, d), BF16), pltpu.VMEM((tm, half), F32)],
        compiler_params=_params("arbitrary", "arbitrary"),
        name="sgu_in",
    )(x, g, mod, w_in, v_g)


def _sgu_out_kernel(u_ref, v_ref, ws_ref, bs_ref, wout_ref, x_ref, mod_ref, o_ref, y_ref,
                    *, groups, gw):
    tm = u_ref.shape[0]
    rows = lax.broadcasted_iota(jnp.int32, (SG_CHUNK, SG_CHUNK), 0)
    cols = lax.broadcasted_iota(jnp.int32, (SG_CHUNK, SG_CHUNK), 1)
    causal = (cols // CHUNK) <= (rows // CHUNK)
    for g in range(groups):
        wg = jnp.where(causal, ws_ref[g], 0.0).astype(BF16)
        c0 = g * gw
        for r0 in range(0, tm, SG_CHUNK):
            mixed = jnp.dot(wg, v_ref[r0:r0 + SG_CHUNK, c0:c0 + gw],
                            preferred_element_type=F32) + bs_ref[g]
            u = u_ref[r0:r0 + SG_CHUNK, c0:c0 + gw].astype(F32)
            y_ref[r0:r0 + SG_CHUNK, c0:c0 + gw] = (u * mixed).astype(BF16)
    y = jnp.dot(y_ref[...], wout_ref[...], preferred_element_type=F32)
    o_ref[...] = x_ref[...] + mod_ref[0, 2:3, :] * y


def sgu_out(z, w_s, b_s, w_out, x, mod, *, seq, tm=256):
    t, d = x.shape
    half = z.shape[1] // 2
    groups = w_s.shape[0]
    per_seq = seq // tm
    return pl.pallas_call(
        functools.partial(_sgu_out_kernel, groups=groups, gw=half // groups),
        out_shape=jax.ShapeDtypeStruct((t, d), F32),
        grid=(t // tm,),
        in_specs=[
            pl.BlockSpec((tm, half), lambda i: (i, 0)),
            pl.BlockSpec((tm, half), lambda i: (i, 1)),
            pl.BlockSpec((groups, SG_CHUNK, SG_CHUNK), lambda i: (0, 0, 0)),
            pl.BlockSpec((groups, SG_CHUNK, 1), lambda i: (0, 0, 0)),
            pl.BlockSpec((half, d), lambda i: (0, 0)),
            pl.BlockSpec((tm, d), lambda i: (i, 0)),
            pl.BlockSpec((1, 6, d), lambda i: (i // per_seq, 0, 0)),
        ],
        out_specs=pl.BlockSpec((tm, d), lambda i: (i, 0)),
        scratch_shapes=[pltpu.VMEM((tm, half), BF16)],
        compiler_params=_params("parallel"),
        name="sgu_out",
    )(z, z, w_s, b_s.reshape(groups, SG_CHUNK, 1), w_out, x, mod)


def _ffn_kernel(x_ref, g_ref, mod_ref, w1_ref, w3_ref, w2_ref, o_ref, h_ref, acc_ref):
    j = pl.program_id(1)

    @pl.when(j == 0)
    def _():
        h_ref[...] = _norm_mod(x_ref[...], g_ref[...], mod_ref[0, 4:5, :],
                               mod_ref[0, 3:4, :]).astype(BF16)
        acc_ref[...] = jnp.zeros_like(acc_ref)

    h = h_ref[...]
    a = jnp.dot(h, w1_ref[...], preferred_element_type=F32)
    b = jnp.dot(h, w3_ref[...], preferred_element_type=F32)
    act = (a * jax.nn.sigmoid(a) * b).astype(BF16)
    acc_ref[...] += jnp.dot(act, w2_ref[...], preferred_element_type=F32)

    @pl.when(j == pl.num_programs(1) - 1)
    def _():
        o_ref[...] = x_ref[...] + mod_ref[0, 5:6, :] * acc_ref[...]


def ffn_dense(x, g, mod, w13, w2, *, seq, tm=512):
    t, d = x.shape
    f = w2.shape[0]
    tf, nf = f, 1
    per_seq = seq // tm
    once = pl.Buffered(1)
    return pl.pallas_call(
        _ffn_kernel,
        out_shape=jax.ShapeDtypeStruct((t, d), F32),
        grid=(t // tm, nf),
        in_specs=[
            pl.BlockSpec((tm, d), lambda i, j: (i, 0)),
            pl.BlockSpec((1, d), lambda i, j: (0, 0)),
            pl.BlockSpec((1, 6, d), lambda i, j: (i // per_seq, 0, 0)),
            pl.BlockSpec((d, tf), lambda i, j: (0, j), pipeline_mode=once),
            pl.BlockSpec((d, tf), lambda i, j: (0, nf + j), pipeline_mode=once),
            pl.BlockSpec((tf, d), lambda i, j: (j, 0), pipeline_mode=once),
        ],
        out_specs=pl.BlockSpec((tm, d), lambda i, j: (i, 0)),
        scratch_shapes=[pltpu.VMEM((tm, d), BF16), pltpu.VMEM((tm, d), F32)],
        compiler_params=_params("parallel", "arbitrary"),
        name="ffn_dense",
    )(x, g, mod, w13, w13, w2)


def _router_kernel(x_ref, g_ref, mod_ref, wr_ref, br_ref, h_ref, sel_ref):
    h = _norm_mod(x_ref[...], g_ref[...], mod_ref[0, 4:5, :], mod_ref[0, 3:4, :])
    h_ref[...] = h
    logits = jnp.dot(h, wr_ref[...], precision=lax.Precision.HIGHEST,
                     preferred_element_type=F32) + br_ref[...]
    lane = lax.broadcasted_iota(jnp.int32, logits.shape, 1).astype(F32)
    m1 = jnp.max(logits, axis=-1, keepdims=True)
    i1 = jnp.min(jnp.where(logits == m1, lane, float(LANES)), axis=-1, keepdims=True)
    rest = jnp.where(lane == i1, NEG_INF, logits)
    m2 = jnp.max(rest, axis=-1, keepdims=True)
    i2 = jnp.min(jnp.where(rest == m2, lane, float(LANES)), axis=-1, keepdims=True)
    e = jnp.exp(m2 - m1)
    w1 = 1.0 / (1.0 + e)
    sel_ref[...] = (jnp.where(lane == 0.0, i1, 0.0) + jnp.where(lane == 1.0, i2, 0.0)
                    + jnp.where(lane == 2.0, w1, 0.0) + jnp.where(lane == 3.0, e * w1, 0.0))


def router(x, g, mod, w_r, b_r, *, seq, tm=1024):
    t, d = x.shape
    ne = w_r.shape[1]
    wr = jnp.zeros((d, LANES), F32).at[:, :ne].set(w_r)
    br = jnp.full((1, LANES), NEG_INF, F32).at[0, :ne].set(b_r)
    per_seq = seq // tm
    return pl.pallas_call(
        _router_kernel,
        out_shape=(jax.ShapeDtypeStruct((t, d), F32), jax.ShapeDtypeStruct((t, LANES), F32)),
        grid=(t // tm,),
        in_specs=[
            pl.BlockSpec((tm, d), lambda i: (i, 0)),
            pl.BlockSpec((1, d), lambda i: (0, 0)),
            pl.BlockSpec((1, 6, d), lambda i: (i // per_seq, 0, 0)),
            pl.BlockSpec((d, LANES), lambda i: (0, 0)),
            pl.BlockSpec((1, LANES), lambda i: (0, 0)),
        ],
        out_specs=(pl.BlockSpec((tm, d), lambda i: (i, 0)),
                   pl.BlockSpec((tm, LANES), lambda i: (i, 0))),
        compiler_params=_params("parallel"),
        name="router",
    )(x, g, mod, wr, br)


def dispatch_plan(sel, *, tm):
    t = sel.shape[0]
    ntiles = 2 * t // tm + N_EXPERTS
    experts = sel[:, :2].astype(jnp.int32)
    onehot = (experts[:, :, None] == jnp.arange(N_EXPERTS)[None, None, :]).astype(jnp.int32)
    per_tok = onehot.sum(axis=1)
    before = jnp.cumsum(per_tok, axis=0) - per_tok
    counts = per_tok.sum(axis=0)
    padded = (counts + tm - 1) // tm * tm
    ends = jnp.cumsum(padded)
    starts = ends - padded
    dest = jnp.take_along_axis(starts[None, :] + before, experts, axis=1)
    tokens = jnp.broadcast_to(jnp.arange(t, dtype=jnp.int32)[:, None], (t, 2))
    src = jnp.zeros((ntiles * tm,), jnp.int32).at[dest.reshape(-1)].set(
        tokens.reshape(-1), unique_indices=True, mode="promise_in_bounds")
    tile_start = jnp.arange(ntiles, dtype=jnp.int32) * tm
    tile_expert = jnp.minimum((tile_start[:, None] >= ends[None, :]).sum(axis=1),
                              N_EXPERTS - 1).astype(jnp.int32)
    tile_valid = (tile_start < ends[-1]).astype(jnp.int32)
    return dest, src, tile_expert, tile_valid


def _row_copy(src_hbm, row, dst_vmem, k, sem):
    return pltpu.make_async_copy(src_hbm.at[pl.ds(row, 1)], dst_vmem.at[pl.ds(k, 1)], sem)


def _gather_rows(idx_ref, base, n, src_hbm, dst_vmem, sem):
    for k in range(n):
        _row_copy(src_hbm, idx_ref[0, 0, base + k], dst_vmem, k, sem).start(priority=1)

    pltpu.make_async_copy(src_hbm.at[pl.ds(0, n)], dst_vmem.at[pl.ds(0, n)], sem).wait()


def _moe_kernel(te_ref, tv_ref, src_ref, h_hbm, w1_ref, w3_ref, w2_ref, o_ref,
                xg_ref, xb_ref, acc_ref, sem):
    r = pl.program_id(0)
    j = pl.program_id(1)
    tm = xg_ref.shape[0]
    live = tv_ref[r] == 1

    @pl.when(live & (j == 0))
    def _():
        _gather_rows(src_ref, 0, tm, h_hbm, xg_ref, sem)
        xb_ref[...] = xg_ref[...].astype(BF16)
        acc_ref[...] = jnp.zeros_like(acc_ref)

    @pl.when(live)
    def _():
        xb = xb_ref[...]
        a = jnp.dot(xb, w1_ref[0, 0], preferred_element_type=F32)
        b = jnp.dot(xb, w3_ref[0, 0], preferred_element_type=F32)
        act = (a * jax.nn.sigmoid(a) * b).astype(BF16)
        acc_ref[...] += jnp.dot(act, w2_ref[0, 0], preferred_element_type=F32)

    @pl.when(j == pl.num_programs(1) - 1)
    def _():
        @pl.when(live)
        def _():
            o_ref[...] = acc_ref[...]

        @pl.when(jnp.logical_not(live))
        def _():
            o_ref[...] = jnp.zeros_like(o_ref)


def moe_experts(h, src, tile_expert, tile_valid, w1, w3, w2, *, layer, tm, tf=1792):
    t, d = h.shape
    f = w1.shape[-1]
    ntiles = tile_expert.shape[0]
    nf = f // tf

    def wcol(r, j, te, tv):
        return jnp.where(tv[r] == 1, j, nf - 1)

    return pl.pallas_call(
        _moe_kernel,
        out_shape=jax.ShapeDtypeStruct((ntiles * tm, d), F32),
        grid_spec=pltpu.PrefetchScalarGridSpec(
            num_scalar_prefetch=2,
            grid=(ntiles, nf),
            in_specs=[
                pl.BlockSpec((1, 1, tm), lambda r, j, te, tv: (r, 0, 0), memory_space=pltpu.SMEM),
                pl.BlockSpec(memory_space=pl.ANY),
                pl.BlockSpec((1, 1, d, tf),
                             lambda r, j, te, tv: (layer, te[r], 0, wcol(r, j, te, tv))),
                pl.BlockSpec((1, 1, d, tf),
                             lambda r, j, te, tv: (layer, te[r], 0, wcol(r, j, te, tv))),
                pl.BlockSpec((1, 1, tf, d),
                             lambda r, j, te, tv: (layer, te[r], wcol(r, j, te, tv), 0)),
            ],
            out_specs=pl.BlockSpec((tm, d), lambda r, j, te, tv: (r, 0)),
            scratch_shapes=[
                pltpu.VMEM((tm, d), F32),
                pltpu.VMEM((tm, d), BF16),
                pltpu.VMEM((tm, d), F32),
                pltpu.SemaphoreType.DMA,
            ],
        ),
        compiler_params=_params("arbitrary", "arbitrary"),
        name="moe_experts",
    )(tile_expert, tile_valid, src.reshape(ntiles, 1, tm), h, w1, w3, w2)


def _combine_kernel(dest_ref, y_hbm, sel_ref, x_ref, mod_ref, o_ref, yg_ref, sem):
    tc = x_ref.shape[0]
    _gather_rows(dest_ref, 0, tc, y_hbm, yg_ref.at[0], sem)
    _gather_rows(dest_ref, tc, tc, y_hbm, yg_ref.at[1], sem)
    sel = sel_ref[...]
    lane = lax.broadcasted_iota(jnp.int32, sel.shape, 1)
    w_first = jnp.sum(jnp.where(lane == 2, sel, 0.0), axis=-1, keepdims=True)
    w_second = jnp.sum(jnp.where(lane == 3, sel, 0.0), axis=-1, keepdims=True)
    y = w_first * yg_ref[0] + w_second * yg_ref[1]
    o_ref[...] = x_ref[...] + mod_ref[0, 5:6, :] * y


def moe_combine(ys, dest, sel, x, mod, *, seq, tc=512):
    t, d = x.shape
    nblk = t // tc
    per_seq = seq // tc
    idx = dest.reshape(nblk, tc, 2).transpose(0, 2, 1).reshape(nblk, 1, 2 * tc)
    return pl.pallas_call(
        _combine_kernel,
        out_shape=jax.ShapeDtypeStruct((t, d), F32),
        grid=(nblk,),
        in_specs=[
            pl.BlockSpec((1, 1, 2 * tc), lambda i: (i, 0, 0), memory_space=pltpu.SMEM),
            pl.BlockSpec(memory_space=pl.ANY),
            pl.BlockSpec((tc, LANES), lambda i: (i, 0)),
            pl.BlockSpec((tc, d), lambda i: (i, 0)),
            pl.BlockSpec((1, 6, d), lambda i: (i // per_seq, 0, 0)),
        ],
        out_specs=pl.BlockSpec((tc, d), lambda i: (i, 0)),
        scratch_shapes=[pltpu.VMEM((2, tc, d), F32), pltpu.SemaphoreType.DMA],
        compiler_params=_params("arbitrary"),
        name="moe_combine",
    )(idx, ys, sel, x, mod)


def kernel(x, c, norm_g, ada_w, ada_b, sb_wqkv, sb_qg, sb_kg, sb_wo, sg_win, sg_vg, sg_ws, sg_bs,
           sg_wout, ca_wqkv, ca_qg, ca_kg, ca_relb, ca_wo, ff_w13, ff_w2, moe_wr, moe_br,
           moe_w1, moe_w3, moe_w2):
    batch, seq, d = x.shape
    depth = norm_g.shape[0]
    mods = ada_mods(c, ada_w, ada_b).reshape(depth, batch, 6, d)
    xt = x.reshape(batch * seq, d)
    bf = lambda w: w.astype(BF16)
    moe_w1b, moe_w3b, moe_w2b = bf(moe_w1), bf(moe_w3), bf(moe_w2)
    for i in range(depth):
        mod = mods[i]
        kind, j = i % 3, i // 3
        g_tok = norm_g[i, 0].reshape(1, d)
        g_ch = norm_g[i, 1].reshape(1, d)
        if kind == 0:
            qkv = norm_proj(xt, g_tok, mod, bf(sb_wqkv[j]), seq=seq, shift_idx=0, scale_idx=1)
            o = sb_attention(qkv, sb_qg[j], sb_kg[j], batch=batch, seq=seq)
            xt = out_proj_residual(o, bf(sb_wo[j]), xt, mod, seq=seq, gate_idx=2)
        elif kind == 1:
            z = sgu_in(xt, g_tok, mod, bf(sg_win[j]), sg_vg[j].reshape(1, -1), seq=seq)
            xt = sgu_out(z, sg_ws[j], sg_bs[j], bf(sg_wout[j]), xt, mod, seq=seq)
        else:
            qkv = norm_proj(xt, g_tok, mod, bf(ca_wqkv[j]), seq=seq, shift_idx=0, scale_idx=1)
            o = ca_attention(qkv, ca_qg[j], ca_kg[j], ca_relb[j], batch=batch, seq=seq)
            xt = out_proj_residual(o, bf(ca_wo[j]), xt, mod, seq=seq, gate_idx=2)
        m = i // 2
        if i % 2 == 0:
            xt = ffn_dense(xt, g_ch, mod, bf(ff_w13[m]), bf(ff_w2[m]), seq=seq)
        else:
            h, sel = router(xt, g_ch, mod, moe_wr[m], moe_br[m], seq=seq)
            dest, src, tile_expert, tile_valid = dispatch_plan(sel, tm=MOE_TM)
            ys = moe_experts(h, src, tile_expert, tile_valid, moe_w1b, moe_w3b, moe_w2b,
                             layer=m, tm=MOE_TM)
            xt = moe_combine(ys, dest, sel, xt, mod, seq=seq)
    return xt.reshape(batch, seq, d)
```

```python
import functools

import jax
import jax.numpy as jnp
from jax import lax
from jax.experimental import pallas as pl
from jax.experimental.pallas import tpu as pltpu

F32 = jnp.float32
BF16 = jnp.bfloat16

EPS = 1e-6
HEAD_DIM = 64
LANES = 128
CHUNK = 64
SG_CHUNK = 128
CA_PREV = 8
REL_CLIP = 128
N_EXPERTS = 8
MOE_TM = 512
LOG2_E = 1.4426950408889634
SB_DEAD_LOG2 = -160.0
NEG_INF = float("-inf")

VMEM_LIMIT_BYTES = 56 * 1024 * 1024


def _params(*semantics):
    return pltpu.CompilerParams(dimension_semantics=semantics, vmem_limit_bytes=VMEM_LIMIT_BYTES)


def _norm_mod(x, g, scale, shift):
    ms = jnp.mean(x * x, axis=-1, keepdims=True)
    y = x * lax.rsqrt(ms + EPS) * g
    return y * (1.0 + scale) + shift


def _head_rms_norm(x, g2):
    lane = lax.broadcasted_iota(jnp.int32, x.shape, 1)
    first = lane < HEAD_DIM
    x2 = x * x
    s0 = jnp.sum(jnp.where(first, x2, 0.0), axis=-1, keepdims=True)
    s1 = jnp.sum(jnp.where(first, 0.0, x2), axis=-1, keepdims=True)
    ms = jnp.where(first, s0, s1) * (1.0 / HEAD_DIM)
    return x * lax.rsqrt(ms + EPS) * g2


def _ada_kernel(c_ref, w_ref, b_ref, o_ref):
    c = c_ref[...]
    cond = c * jax.nn.sigmoid(c)
    o_ref[0] = jnp.dot(cond, w_ref[0], precision=lax.Precision.HIGHEST,
                       preferred_element_type=F32) + b_ref[0]


def ada_mods(c, ada_w, ada_b, *, tn=1536):
    depth, d, n = ada_w.shape
    b = c.shape[0]
    return pl.pallas_call(
        _ada_kernel,
        out_shape=jax.ShapeDtypeStruct((depth, b, n), F32),
        grid=(depth, n // tn),
        in_specs=[
            pl.BlockSpec((b, d), lambda l, j: (0, 0)),
            pl.BlockSpec((1, d, tn), lambda l, j: (l, 0, j)),
            pl.BlockSpec((1, 1, tn), lambda l, j: (l, 0, j)),
        ],
        out_specs=pl.BlockSpec((1, b, tn), lambda l, j: (l, 0, j)),
        compiler_params=_params("arbitrary", "arbitrary"),
        name="ada_mods",
    )(c, ada_w, ada_b.reshape(depth, 1, n))


def _proj_kernel(x_ref, g_ref, mod_ref, w_ref, o_ref, *, shift_idx, scale_idx, nc):
    h = _norm_mod(x_ref[...], g_ref[...], mod_ref[0, scale_idx:scale_idx + 1, :],
                  mod_ref[0, shift_idx:shift_idx + 1, :]).astype(BF16)
    n = o_ref.shape[1]
    for c0 in range(0, n, nc):
        o_ref[:, c0:c0 + nc] = jnp.dot(h, w_ref[:, c0:c0 + nc],
                                       preferred_element_type=F32).astype(BF16)


def norm_proj(x, g, mod, w, *, seq, shift_idx, scale_idx, tm=512):
    t, d = x.shape
    n = w.shape[1]
    per_seq = seq // tm
    return pl.pallas_call(
        functools.partial(_proj_kernel, shift_idx=shift_idx, scale_idx=scale_idx, nc=1024),
        out_shape=jax.ShapeDtypeStruct((t, n), BF16),
        grid=(t // tm,),
        in_specs=[
            pl.BlockSpec((tm, d), lambda i: (i, 0)),
            pl.BlockSpec((1, d), lambda i: (0, 0)),
            pl.BlockSpec((1, 6, d), lambda i: (i // per_seq, 0, 0)),
            pl.BlockSpec((d, n), lambda i: (0, 0)),
        ],
        out_specs=pl.BlockSpec((tm, n), lambda i: (i, 0)),
        compiler_params=_params("parallel"),
        name="norm_proj",
    )(x, g, mod, w)


def _out_proj_kernel(o_ref, w_ref, x_ref, mod_ref, y_ref, *, gate_idx):
    y = jnp.dot(o_ref[...], w_ref[...], preferred_element_type=F32)
    y_ref[...] = x_ref[...] + mod_ref[0, gate_idx:gate_idx + 1, :] * y


def out_proj_residual(o, w, x, mod, *, seq, gate_idx, tm=1024):
    t, d = x.shape
    k = o.shape[1]
    per_seq = seq // tm
    return pl.pallas_call(
        functools.partial(_out_proj_kernel, gate_idx=gate_idx),
        out_shape=jax.ShapeDtypeStruct((t, d), F32),
        grid=(t // tm,),
        in_specs=[
            pl.BlockSpec((tm, k), lambda i: (i, 0)),
            pl.BlockSpec((k, d), lambda i: (0, 0)),
            pl.BlockSpec((tm, d), lambda i: (i, 0)),
            pl.BlockSpec((1, 6, d), lambda i: (i // per_seq, 0, 0)),
        ],
        out_specs=pl.BlockSpec((tm, d), lambda i: (i, 0)),
        compiler_params=_params("parallel"),
        name="out_proj_residual",
    )(o, w, x, mod)


def _stack_heads(q, qs_ref):
    tq = q.shape[0]
    first = lax.broadcasted_iota(jnp.int32, q.shape, 1) < HEAD_DIM
    qs_ref[0:tq, :] = jnp.where(first, q, 0.0).astype(BF16)
    qs_ref[tq:2 * tq, :] = jnp.where(first, 0.0, q).astype(BF16)


def _unstack_heads(res, tq):
    first = lax.broadcasted_iota(jnp.int32, (tq, LANES), 1) < HEAD_DIM
    return jnp.where(first, res[0:tq, :], res[tq:2 * tq, :])


def _sb_kernel(q_ref, k_ref, v_ref, qg_ref, kg_ref, tri_ref, o_ref,
               kn_ref, qs_ref, acc_ref, carry_ref, *, tq, tk, scale):
    i = pl.program_id(2)
    gp = qs_ref.shape[0]

    @pl.when(i == 0)
    def _():
        for g in range(gp):
            cols = slice(g * LANES, (g + 1) * LANES)
            kn_ref[:, cols] = _head_rms_norm(k_ref[:, cols].astype(F32), kg_ref[...]).astype(BF16)

    for g in range(gp):
        cols = slice(g * LANES, (g + 1) * LANES)
        _stack_heads(_head_rms_norm(q_ref[:, cols].astype(F32), qg_ref[...]) * scale, qs_ref.at[g])
    acc_ref[...] = jnp.zeros_like(acc_ref)
    carry_ref[...] = jnp.zeros_like(carry_ref)

    def block(j, diag):
        k0 = pl.multiple_of(j * tk, tk)
        if diag:
            row = lax.broadcasted_iota(jnp.int32, (2 * tq, tk), 0)
            qpos = jnp.where(row >= tq, row - tq, row)
            strict = lax.broadcasted_iota(jnp.int32, (2 * tq, tk), 1) < qpos
        for g in range(gp):
            cols = slice(g * LANES, (g + 1) * LANES)
            z = lax.dot_general(qs_ref[g], kn_ref[pl.ds(k0, tk), cols], (((1,), (1,)), ((), ())),
                                preferred_element_type=F32)
            neg = jnp.minimum(z, 0.0)
            log_beta = neg - jnp.log2(1.0 + jnp.exp2(neg + neg - z))
            log_fail = log_beta - z
            if diag:
                log_fail = jnp.where(strict, log_fail, 0.0)
            after = jnp.dot(log_fail.astype(BF16), tri_ref[...], preferred_element_type=F32)
            carry = carry_ref[g]
            a = jnp.exp2(log_beta + after + carry)
            if diag:
                a = jnp.where(strict, a, 0.0)
            acc_ref[g] += jnp.dot(a.astype(BF16), v_ref[pl.ds(k0, tk), cols],
                                  preferred_element_type=F32)
            carry_ref[g] = carry + jnp.sum(log_fail, axis=-1, keepdims=True)

    block(i, True)

    def more(t):
        return jnp.logical_and(t < i, jnp.max(carry_ref[...]) > SB_DEAD_LOG2)

    def body(t):
        block(i - 1 - t, False)
        return t + 1

    lax.while_loop(more, body, 0)
    for g in range(gp):
        o_ref[:, g * LANES:(g + 1) * LANES] = _unstack_heads(acc_ref[g], tq).astype(BF16)


def sb_attention(qkv, q_g, k_g, *, batch, seq, tq=256, gp=4):
    t, n3 = qkv.shape
    d = n3 // 3
    groups = d // LANES // gp
    gw = gp * LANES
    tk = tq
    nq = seq // tq
    idx = jnp.arange(tk)
    tri = (idx[:, None] > idx[None, :]).astype(BF16)
    qg2 = jnp.concatenate([q_g, q_g]).reshape(1, LANES)
    kg2 = jnp.concatenate([k_g, k_g]).reshape(1, LANES)
    return pl.pallas_call(
        functools.partial(_sb_kernel, tq=tq, tk=tk, scale=HEAD_DIM ** -0.5 * LOG2_E),
        out_shape=jax.ShapeDtypeStruct((t, d), BF16),
        grid=(batch, groups, nq),
        in_specs=[
            pl.BlockSpec((tq, gw), lambda b, p, i: (b * nq + i, p)),
            pl.BlockSpec((seq, gw), lambda b, p, i: (b, groups + p)),
            pl.BlockSpec((seq, gw), lambda b, p, i: (b, 2 * groups + p)),
            pl.BlockSpec((1, LANES), lambda b, p, i: (0, 0)),
            pl.BlockSpec((1, LANES), lambda b, p, i: (0, 0)),
            pl.BlockSpec((tk, tk), lambda b, p, i: (0, 0)),
        ],
        out_specs=pl.BlockSpec((tq, gw), lambda b, p, i: (b * nq + i, p)),
        scratch_shapes=[
            pltpu.VMEM((seq, gw), BF16),
            pltpu.VMEM((gp, 2 * tq, LANES), BF16),
            pltpu.VMEM((gp, 2 * tq, LANES), F32),
            pltpu.VMEM((gp, 2 * tq, 1), F32),
        ],
        compiler_params=_params("parallel", "parallel", "arbitrary"),
        name="sb_attention",
    )(qkv, qkv, qkv, qg2, kg2, tri)


def _ca_kernel(q_ref, k_ref, v_ref, qg_ref, kg_ref, bias_ref, o_ref, kn_ref, qs_ref,
               *, tq, nblk, scale):
    i = pl.program_id(2)
    gp = qs_ref.shape[0]

    @pl.when(i == 0)
    def _():
        for g in range(gp):
            cols = slice(g * LANES, (g + 1) * LANES)
            kn_ref[:, cols] = _head_rms_norm(k_ref[:, cols].astype(F32), kg_ref[...]).astype(BF16)

    starts, live = [], []
    for o in range(nblk):
        jb = i - (nblk - 1) + o
        starts.append(pl.multiple_of(jnp.maximum(jb, 0) * tq, tq))
        live.append(jb >= 0)

    for g in range(gp):
        cols = slice(g * LANES, (g + 1) * LANES)
        _stack_heads(_head_rms_norm(q_ref[:, cols].astype(F32), qg_ref[...]) * scale, qs_ref.at[g])
        qs = qs_ref[g]
        scores = []
        for o in range(nblk):
            z = lax.dot_general(qs, kn_ref[pl.ds(starts[o], tq), cols], (((1,), (1,)), ((), ())),
                                preferred_element_type=F32)
            z = z + bias_ref[g, :, o * tq:(o + 1) * tq]
            scores.append(jnp.where(live[o], z, NEG_INF))
        m_lanes = scores[0]
        for o in range(1, nblk):
            m_lanes = jnp.maximum(m_lanes, scores[o])
        m = jnp.max(m_lanes, axis=-1, keepdims=True)
        l_lanes = jnp.zeros((2 * tq, tq), F32)
        acc = jnp.zeros((2 * tq, LANES), F32)
        for o in range(nblk):
            p = jnp.exp2(scores[o] - m)
            l_lanes = l_lanes + p
            acc = acc + jnp.dot(p.astype(BF16), v_ref[pl.ds(starts[o], tq), cols],
                                preferred_element_type=F32)
        l = jnp.sum(l_lanes, axis=-1, keepdims=True)
        o_ref[:, cols] = _unstack_heads(acc / l, tq).astype(BF16)


def ca_attention(qkv, q_g, k_g, rel_bias, *, batch, seq, tq=128, gp=8):
    t, n3 = qkv.shape
    d = n3 // 3
    pairs = d // LANES
    groups = pairs // gp
    gw = gp * LANES
    nq = seq // tq
    pad = CA_PREV * CHUNK
    nblk = pad // tq + 1
    band = nblk * tq
    heads = rel_bias.shape[0]
    span = tq + band - 1
    rel_of_m = pad - ((jnp.arange(span) + tq - 1) % span - (tq - 1))
    row = rel_bias[:, jnp.clip(rel_of_m, -REL_CLIP, REL_CLIP) + REL_CLIP].astype(F32)
    bias = jnp.tile(row, (1, tq))[:, :tq * (span - 1)].reshape(heads, tq, span - 1)[:, :, :band]
    qi = jnp.arange(tq)[:, None]
    kj = jnp.arange(band)[None, :]
    qc = qi // CHUNK
    kc = (kj - pad) // CHUNK
    visible = (kc <= qc) & (kc >= qc - CA_PREV)
    bias = jnp.where(visible[None], bias * LOG2_E, NEG_INF).reshape(heads // 2, 2 * tq, band)
    qg2 = jnp.concatenate([q_g, q_g]).reshape(1, LANES)
    kg2 = jnp.concatenate([k_g, k_g]).reshape(1, LANES)
    return pl.pallas_call(
        functools.partial(_ca_kernel, tq=tq, nblk=nblk, scale=HEAD_DIM ** -0.5 * LOG2_E),
        out_shape=jax.ShapeDtypeStruct((t, d), BF16),
        grid=(batch, groups, nq),
        in_specs=[
            pl.BlockSpec((tq, gw), lambda b, p, i: (b * nq + i, p)),
            pl.BlockSpec((seq, gw), lambda b, p, i: (b, groups + p)),
            pl.BlockSpec((seq, gw), lambda b, p, i: (b, 2 * groups + p)),
            pl.BlockSpec((1, LANES), lambda b, p, i: (0, 0)),
            pl.BlockSpec((1, LANES), lambda b, p, i: (0, 0)),
            pl.BlockSpec((gp, 2 * tq, band), lambda b, p, i: (p, 0, 0)),
        ],
        out_specs=pl.BlockSpec((tq, gw), lambda b, p, i: (b * nq + i, p)),
        scratch_shapes=[pltpu.VMEM((seq, gw), BF16), pltpu.VMEM((gp, 2 * tq, LANES), BF16)],
        compiler_params=_params("parallel", "parallel", "arbitrary"),
        name="ca_attention",
    )(qkv, qkv, qkv, qg2, kg2, bias)


def _sgu_in_kernel(x_ref, g_ref, mod_ref, w_ref, vg_ref, o_ref, h_ref, z_ref, *, nc):
    j = pl.program_id(0)
    n = o_ref.shape[1]
    h_ref[...] = _norm_mod(x_ref[...], g_ref[...], mod_ref[0, 1:2, :],
                           mod_ref[0, 0:1, :]).astype(BF16)

    @pl.when(j == 0)
    def _():
        for c0 in range(0, n, nc):
            acc = jnp.dot(h_ref[...], w_ref[:, c0:c0 + nc], preferred_element_type=F32)
            o_ref[:, c0:c0 + nc] = jax.nn.gelu(acc).astype(BF16)

    @pl.when(j == 1)
    def _():
        ss = jnp.zeros((o_ref.shape[0], 1), F32)
        for c0 in range(0, n, nc):
            acc = jnp.dot(h_ref[...], w_ref[:, c0:c0 + nc], preferred_element_type=F32)
            z = jax.nn.gelu(acc)
            z_ref[:, c0:c0 + nc] = z
            ss = ss + jnp.sum(z * z, axis=-1, keepdims=True)
        rstd = lax.rsqrt(ss * (1.0 / n) + EPS)
        o_ref[...] = (z_ref[...] * rstd * vg_ref[...]).astype(BF16)


def sgu_in(x, g, mod, w_in, v_g, *, seq, tm=512):
    t, d = x.shape
    half = w_in.shape[1] // 2
    per_seq = seq // tm
    return pl.pallas_call(
        functools.partial(_sgu_in_kernel, nc=1024),
        out_shape=jax.ShapeDtypeStruct((t, 2 * half), BF16),
        grid=(2, t // tm),
        in_specs=[
            pl.BlockSpec((tm, d), lambda j, i: (i, 0)),
            pl.BlockSpec((1, d), lambda j, i: (0, 0)),
            pl.BlockSpec((1, 6, d), lambda j, i: (i // per_seq, 0, 0)),
            pl.BlockSpec((d, half), lambda j, i: (0, j)),
            pl.BlockSpec((1, half), lambda j, i: (0, 0)),
        ],
        out_specs=pl.BlockSpec((tm, half), lambda j, i: (i, j)),
        scratch_shapes=[pltpu.VMEM((tm, d), BF16), pltpu.VMEM((tm, half), F32)],
        compiler_params=_params("arbitrary", "arbitrary"),
        name="sgu_in",
    )(x, g, mod, w_in, v_g)


def _sgu_out_kernel(u_ref, v_ref, ws_ref, bs_ref, wout_ref, x_ref, mod_ref, o_ref, y_ref,
                    *, groups, gw):
    tm = u_ref.shape[0]
    rows = lax.broadcasted_iota(jnp.int32, (SG_CHUNK, SG_CHUNK), 0)
    cols = lax.broadcasted_iota(jnp.int32, (SG_CHUNK, SG_CHUNK), 1)
    causal = (cols // CHUNK) <= (rows // CHUNK)
    for g in range(groups):
        wg = jnp.where(causal, ws_ref[g], 0.0).astype(BF16)
        c0 = g * gw
        for r0 in range(0, tm, SG_CHUNK):
            mixed = jnp.dot(wg, v_ref[r0:r0 + SG_CHUNK, c0:c0 + gw],
                            preferred_element_type=F32) + bs_ref[g]
            u = u_ref[r0:r0 + SG_CHUNK, c0:c0 + gw].astype(F32)
            y_ref[r0:r0 + SG_CHUNK, c0:c0 + gw] = (u * mixed).astype(BF16)
    y = jnp.dot(y_ref[...], wout_ref[...], preferred_element_type=F32)
    o_ref[...] = x_ref[...] + mod_ref[0, 2:3, :] * y


def sgu_out(z, w_s, b_s, w_out, x, mod, *, seq, tm=256):
    t, d = x.shape
    half = z.shape[1] // 2
    groups = w_s.shape[0]
    per_seq = seq // tm
    return pl.pallas_call(
        functools.partial(_sgu_out_kernel, groups=groups, gw=half // groups),
        out_shape=jax.ShapeDtypeStruct((t, d), F32),
        grid=(t // tm,),
        in_specs=[
            pl.BlockSpec((tm, half), lambda i: (i, 0)),
            pl.BlockSpec((tm, half), lambda i: (i, 1)),
            pl.BlockSpec((groups, SG_CHUNK, SG_CHUNK), lambda i: (0, 0, 0)),
            pl.BlockSpec((groups, SG_CHUNK, 1), lambda i: (0, 0, 0)),
            pl.BlockSpec((half, d), lambda i: (0, 0)),
            pl.BlockSpec((tm, d), lambda i: (i, 0)),
            pl.BlockSpec((1, 6, d), lambda i: (i // per_seq, 0, 0)),
        ],
        out_specs=pl.BlockSpec((tm, d), lambda i: (i, 0)),
        scratch_shapes=[pltpu.VMEM((tm, half), BF16)],
        compiler_params=_params("parallel"),
        name="sgu_out",
    )(z, z, w_s, b_s.reshape(groups, SG_CHUNK, 1), w_out, x, mod)


def _ffn_kernel(x_ref, g_ref, mod_ref, w1_ref, w3_ref, w2_ref, o_ref, h_ref, acc_ref):
    j = pl.program_id(1)

    @pl.when(j == 0)
    def _():
        h_ref[...] = _norm_mod(x_ref[...], g_ref[...], mod_ref[0, 4:5, :],
                               mod_ref[0, 3:4, :]).astype(BF16)
        acc_ref[...] = jnp.zeros_like(acc_ref)

    h = h_ref[...]
    a = jnp.dot(h, w1_ref[...], preferred_element_type=F32)
    b = jnp.dot(h, w3_ref[...], preferred_element_type=F32)
    act = (a * jax.nn.sigmoid(a) * b).astype(BF16)
    acc_ref[...] += jnp.dot(act, w2_ref[...], preferred_element_type=F32)

    @pl.when(j == pl.num_programs(1) - 1)
    def _():
        o_ref[...] = x_ref[...] + mod_ref[0, 5:6, :] * acc_ref[...]


def ffn_dense(x, g, mod, w13, w2, *, seq, tm=512):
    t, d = x.shape
    f = w2.shape[0]
    tf, nf = f, 1
    per_seq = seq // tm
    once = pl.Buffered(1)
    return pl.pallas_call(
        _ffn_kernel,
        out_shape=jax.ShapeDtypeStruct((t, d), F32),
        grid=(t // tm, nf),
        in_specs=[
            pl.BlockSpec((tm, d), lambda i, j: (i, 0)),
            pl.BlockSpec((1, d), lambda i, j: (0, 0)),
            pl.BlockSpec((1, 6, d), lambda i, j: (i // per_seq, 0, 0)),
            pl.BlockSpec((d, tf), lambda i, j: (0, j), pipeline_mode=once),
            pl.BlockSpec((d, tf), lambda i, j: (0, nf + j), pipeline_mode=once),
            pl.BlockSpec((tf, d), lambda i, j: (j, 0), pipeline_mode=once),
        ],
        out_specs=pl.BlockSpec((tm, d), lambda i, j: (i, 0)),
        scratch_shapes=[pltpu.VMEM((tm, d), BF16), pltpu.VMEM((tm, d), F32)],
        compiler_params=_params("parallel", "arbitrary"),
        name="ffn_dense",
    )(x, g, mod, w13, w13, w2)


def _router_kernel(x_ref, g_ref, mod_ref, wr_ref, br_ref, h_ref, sel_ref):
    h = _norm_mod(x_ref[...], g_ref[...], mod_ref[0, 4:5, :], mod_ref[0, 3:4, :])
    h_ref[...] = h
    logits = jnp.dot(h, wr_ref[...], precision=lax.Precision.HIGHEST,
                     preferred_element_type=F32) + br_ref[...]
    lane = lax.broadcasted_iota(jnp.int32, logits.shape, 1).astype(F32)
    m1 = jnp.max(logits, axis=-1, keepdims=True)
    i1 = jnp.min(jnp.where(logits == m1, lane, float(LANES)), axis=-1, keepdims=True)
    rest = jnp.where(lane == i1, NEG_INF, logits)
    m2 = jnp.max(rest, axis=-1, keepdims=True)
    i2 = jnp.min(jnp.where(rest == m2, lane, float(LANES)), axis=-1, keepdims=True)
    e = jnp.exp(m2 - m1)
    w1 = 1.0 / (1.0 + e)
    sel_ref[...] = (jnp.where(lane == 0.0, i1, 0.0) + jnp.where(lane == 1.0, i2, 0.0)
                    + jnp.where(lane == 2.0, w1, 0.0) + jnp.where(lane == 3.0, e * w1, 0.0))


def router(x, g, mod, w_r, b_r, *, seq, tm=1024):
    t, d = x.shape
    ne = w_r.shape[1]
    wr = jnp.zeros((d, LANES), F32).at[:, :ne].set(w_r)
    br = jnp.full((1, LANES), NEG_INF, F32).at[0, :ne].set(b_r)
    per_seq = seq // tm
    return pl.pallas_call(
        _router_kernel,
        out_shape=(jax.ShapeDtypeStruct((t, d), F32), jax.ShapeDtypeStruct((t, LANES), F32)),
        grid=(t // tm,),
        in_specs=[
            pl.BlockSpec((tm, d), lambda i: (i, 0)),
            pl.BlockSpec((1, d), lambda i: (0, 0)),
            pl.BlockSpec((1, 6, d), lambda i: (i // per_seq, 0, 0)),
            pl.BlockSpec((d, LANES), lambda i: (0, 0)),
            pl.BlockSpec((1, LANES), lambda i: (0, 0)),
        ],
        out_specs=(pl.BlockSpec((tm, d), lambda i: (i, 0)),
                   pl.BlockSpec((tm, LANES), lambda i: (i, 0))),
        compiler_params=_params("parallel"),
        name="router",
    )(x, g, mod, wr, br)


def dispatch_plan(sel, *, tm):
    t = sel.shape[0]
    ntiles = 2 * t // tm + N_EXPERTS
    experts = sel[:, :2].astype(jnp.int32)
    onehot = (experts[:, :, None] == jnp.arange(N_EXPERTS)[None, None, :]).astype(jnp.int32)
    per_tok = onehot.sum(axis=1)
    before = jnp.cumsum(per_tok, axis=0) - per_tok
    counts = per_tok.sum(axis=0)
    padded = (counts + tm - 1) // tm * tm
    ends = jnp.cumsum(padded)
    starts = ends - padded
    dest = jnp.take_along_axis(starts[None, :] + before, experts, axis=1)
    tokens = jnp.broadcast_to(jnp.arange(t, dtype=jnp.int32)[:, None], (t, 2))
    src = jnp.zeros((ntiles * tm,), jnp.int32).at[dest.reshape(-1)].set(
        tokens.reshape(-1), unique_indices=True, mode="promise_in_bounds")
    tile_start = jnp.arange(ntiles, dtype=jnp.int32) * tm
    tile_expert = jnp.minimum((tile_start[:, None] >= ends[None, :]).sum(axis=1),
                              N_EXPERTS - 1).astype(jnp.int32)
    tile_valid = (tile_start < ends[-1]).astype(jnp.int32)
    return dest, src, tile_expert, tile_valid


def _row_copy(src_hbm, row, dst_vmem, k, sem):
    return pltpu.make_async_copy(src_hbm.at[pl.ds(row, 1)], dst_vmem.at[pl.ds(k, 1)], sem)


def _start_rows(idx_ref, base, n, src_hbm, dst_vmem, sem):
    for k in range(n):
        _row_copy(src_hbm, idx_ref[0, 0, base + k], dst_vmem, k, sem).start(priority=1)


def _wait_rows(n, src_hbm, dst_vmem, sem):
    pltpu.make_async_copy(src_hbm.at[pl.ds(0, n)], dst_vmem.at[pl.ds(0, n)], sem).wait()


def _gather_rows(idx_ref, base, n, src_hbm, dst_vmem, sem):
    _start_rows(idx_ref, base, n, src_hbm, dst_vmem, sem)
    _wait_rows(n, src_hbm, dst_vmem, sem)


def _moe_kernel(te_ref, tv_ref, src_ref, nxt_ref, h_hbm, w1_ref, w3_ref, w2_ref, o_ref,
                xg_ref, xb_ref, acc_ref, sem):
    r = pl.program_id(0)
    j = pl.program_id(1)
    last_j = pl.num_programs(1) - 1
    tm = xb_ref.shape[0]
    live = tv_ref[r] == 1
    nxt = jnp.minimum(r + 1, pl.num_programs(0) - 1)
    nxt_live = (r + 1 < pl.num_programs(0)) & (tv_ref[nxt] == 1)

    for slot in range(2):
        @pl.when(live & (j == 0) & (r % 2 == slot))
        def _(slot=slot):
            if slot == 0:
                @pl.when(r == 0)
                def _():
                    _start_rows(src_ref, 0, tm, h_hbm, xg_ref.at[0], sem.at[0])
            _wait_rows(tm, h_hbm, xg_ref.at[slot], sem.at[slot])
            xb_ref[...] = xg_ref[slot].astype(BF16)
            acc_ref[...] = jnp.zeros_like(acc_ref)

        @pl.when(live & nxt_live & (j == last_j) & (r % 2 == slot))
        def _(slot=slot):
            _start_rows(nxt_ref, 0, tm, h_hbm, xg_ref.at[1 - slot], sem.at[1 - slot])

    @pl.when(live)
    def _():
        xb = xb_ref[...]
        a = jnp.dot(xb, w1_ref[0, 0], preferred_element_type=F32)
        b = jnp.dot(xb, w3_ref[0, 0], preferred_element_type=F32)
        act = (a * jax.nn.sigmoid(a) * b).astype(BF16)
        acc_ref[...] += jnp.dot(act, w2_ref[0, 0], preferred_element_type=F32)

    @pl.when(j == last_j)
    def _():
        @pl.when(live)
        def _():
            o_ref[...] = acc_ref[...]

        @pl.when(jnp.logical_not(live))
        def _():
            o_ref[...] = jnp.zeros_like(o_ref)


def moe_experts(h, src, tile_expert, tile_valid, w1, w3, w2, *, layer, tm, tf=1792):
    t, d = h.shape
    f = w1.shape[-1]
    ntiles = tile_expert.shape[0]
    nf = f // tf
    src3 = src.reshape(ntiles, 1, tm)

    def wcol(r, j, te, tv):
        return jnp.where(tv[r] == 1, j, nf - 1)

    return pl.pallas_call(
        _moe_kernel,
        out_shape=jax.ShapeDtypeStruct((ntiles * tm, d), F32),
        grid_spec=pltpu.PrefetchScalarGridSpec(
            num_scalar_prefetch=2,
            grid=(ntiles, nf),
            in_specs=[
                pl.BlockSpec((1, 1, tm), lambda r, j, te, tv: (r, 0, 0), memory_space=pltpu.SMEM),
                pl.BlockSpec((1, 1, tm), lambda r, j, te, tv: (jnp.minimum(r + 1, ntiles - 1), 0, 0),
                             memory_space=pltpu.SMEM),
                pl.BlockSpec(memory_space=pl.ANY),
                pl.BlockSpec((1, 1, d, tf),
                             lambda r, j, te, tv: (layer, te[r], 0, wcol(r, j, te, tv))),
                pl.BlockSpec((1, 1, d, tf),
                             lambda r, j, te, tv: (layer, te[r], 0, wcol(r, j, te, tv))),
                pl.BlockSpec((1, 1, tf, d),
                             lambda r, j, te, tv: (layer, te[r], wcol(r, j, te, tv), 0)),
            ],
            out_specs=pl.BlockSpec((tm, d), lambda r, j, te, tv: (r, 0)),
            scratch_shapes=[
                pltpu.VMEM((2, tm, d), F32),
                pltpu.VMEM((tm, d), BF16),
                pltpu.VMEM((tm, d), F32),
                pltpu.SemaphoreType.DMA((2,)),
            ],
        ),
        compiler_params=_params("arbitrary", "arbitrary"),
        name="moe_experts",
    )(tile_expert, tile_valid, src3, src3, h, w1, w3, w2)


def _combine_kernel(dest_ref, y_hbm, sel_ref, x_ref, mod_ref, o_ref, yg_ref, sem):
    tc = x_ref.shape[0]
    _gather_rows(dest_ref, 0, tc, y_hbm, yg_ref.at[0], sem)
    _gather_rows(dest_ref, tc, tc, y_hbm, yg_ref.at[1], sem)
    sel = sel_ref[...]
    lane = lax.broadcasted_iota(jnp.int32, sel.shape, 1)
    w_first = jnp.sum(jnp.where(lane == 2, sel, 0.0), axis=-1, keepdims=True)
    w_second = jnp.sum(jnp.where(lane == 3, sel, 0.0), axis=-1, keepdims=True)
    y = w_first * yg_ref[0] + w_second * yg_ref[1]
    o_ref[...] = x_ref[...] + mod_ref[0, 5:6, :] * y


def moe_combine(ys, dest, sel, x, mod, *, seq, tc=512):
    t, d = x.shape
    nblk = t // tc
    per_seq = seq // tc
    idx = dest.reshape(nblk, tc, 2).transpose(0, 2, 1).reshape(nblk, 1, 2 * tc)
    return pl.pallas_call(
        _combine_kernel,
        out_shape=jax.ShapeDtypeStruct((t, d), F32),
        grid=(nblk,),
        in_specs=[
            pl.BlockSpec((1, 1, 2 * tc), lambda i: (i, 0, 0), memory_space=pltpu.SMEM),
            pl.BlockSpec(memory_space=pl.ANY),
            pl.BlockSpec((tc, LANES), lambda i: (i, 0)),
            pl.BlockSpec((tc, d), lambda i: (i, 0)),
            pl.BlockSpec((1, 6, d), lambda i: (i // per_seq, 0, 0)),
        ],
        out_specs=pl.BlockSpec((tc, d), lambda i: (i, 0)),
        scratch_shapes=[pltpu.VMEM((2, tc, d), F32), pltpu.SemaphoreType.DMA],
        compiler_params=_params("arbitrary"),
        name="moe_combine",
    )(idx, ys, sel, x, mod)


def kernel(x, c, norm_g, ada_w, ada_b, sb_wqkv, sb_qg, sb_kg, sb_wo, sg_win, sg_vg, sg_ws, sg_bs,
           sg_wout, ca_wqkv, ca_qg, ca_kg, ca_relb, ca_wo, ff_w13, ff_w2, moe_wr, moe_br,
           moe_w1, moe_w3, moe_w2):
    batch, seq, d = x.shape
    depth = norm_g.shape[0]
    mods = ada_mods(c, ada_w, ada_b).reshape(depth, batch, 6, d)
    xt = x.reshape(batch * seq, d)
    bf = lambda w: w.astype(BF16)
    moe_w1b, moe_w3b, moe_w2b = bf(moe_w1), bf(moe_w3), bf(moe_w2)
    for i in range(depth):
        mod = mods[i]
        kind, j = i % 3, i // 3
        g_tok = norm_g[i, 0].reshape(1, d)
        g_ch = norm_g[i, 1].reshape(1, d)
        if kind == 0:
            qkv = norm_proj(xt, g_tok, mod, bf(sb_wqkv[j]), seq=seq, shift_idx=0, scale_idx=1)
            o = sb_attention(qkv, sb_qg[j], sb_kg[j], batch=batch, seq=seq)
            xt = out_proj_residual(o, bf(sb_wo[j]), xt, mod, seq=seq, gate_idx=2)
        elif kind == 1:
            z = sgu_in(xt, g_tok, mod, bf(sg_win[j]), sg_vg[j].reshape(1, -1), seq=seq)
            xt = sgu_out(z, sg_ws[j], sg_bs[j], bf(sg_wout[j]), xt, mod, seq=seq)
        else:
            qkv = norm_proj(xt, g_tok, mod, bf(ca_wqkv[j]), seq=seq, shift_idx=0, scale_idx=1)
            o = ca_attention(qkv, ca_qg[j], ca_kg[j], ca_relb[j], batch=batch, seq=seq)
            xt = out_proj_residual(o, bf(ca_wo[j]), xt, mod, seq=seq, gate_idx=2)
        m = i // 2
        if i % 2 == 0:
            xt = ffn_dense(xt, g_ch, mod, bf(ff_w13[m]), bf(ff_w2[m]), seq=seq)
        else:
            h, sel = router(xt, g_ch, mod, moe_wr[m], moe_br[m], seq=seq)
            dest, src, tile_expert, tile_valid = dispatch_plan(sel, tm=MOE_TM)
            ys = moe_experts(h, src, tile_expert, tile_valid, moe_w1b, moe_w3b, moe_w2b,
                             layer=m, tm=MOE_TM)
            xt = moe_combine(ys, dest, sel, xt, mod, seq=seq)
    return xt.reshape(batch, seq, d)
```

```python
import functools

import jax
import jax.numpy as jnp
from jax import lax
from jax.experimental import pallas as pl
from jax.experimental.pallas import tpu as pltpu

F32 = jnp.float32
BF16 = jnp.bfloat16

EPS = 1e-6
HEAD_DIM = 64
LANES = 128
CHUNK = 64
SG_CHUNK = 128
CA_PREV = 8
REL_CLIP = 128
N_EXPERTS = 8
MOE_TM = 512
LOG2_E = 1.4426950408889634
SB_DEAD_LOG2 = -160.0
NEG_INF = float("-inf")

VMEM_LIMIT_BYTES = 56 * 1024 * 1024


def _params(*semantics):
    return pltpu.CompilerParams(dimension_semantics=semantics, vmem_limit_bytes=VMEM_LIMIT_BYTES)


def _norm_mod(x, g, scale, shift):
    ms = jnp.mean(x * x, axis=-1, keepdims=True)
    y = x * lax.rsqrt(ms + EPS) * g
    return y * (1.0 + scale) + shift


def _head_rms_norm(x, g2):
    lane = lax.broadcasted_iota(jnp.int32, x.shape, 1)
    first = lane < HEAD_DIM
    x2 = x * x
    s0 = jnp.sum(jnp.where(first, x2, 0.0), axis=-1, keepdims=True)
    s1 = jnp.sum(jnp.where(first, 0.0, x2), axis=-1, keepdims=True)
    ms = jnp.where(first, s0, s1) * (1.0 / HEAD_DIM)
    return x * lax.rsqrt(ms + EPS) * g2


def _ada_kernel(c_ref, w_ref, b_ref, o_ref):
    c = c_ref[...]
    cond = c * jax.nn.sigmoid(c)
    o_ref[0] = jnp.dot(cond, w_ref[0], precision=lax.Precision.HIGHEST,
                       preferred_element_type=F32) + b_ref[0]


def ada_mods(c, ada_w, ada_b, *, tn=1536):
    depth, d, n = ada_w.shape
    b = c.shape[0]
    return pl.pallas_call(
        _ada_kernel,
        out_shape=jax.ShapeDtypeStruct((depth, b, n), F32),
        grid=(depth, n // tn),
        in_specs=[
            pl.BlockSpec((b, d), lambda l, j: (0, 0)),
            pl.BlockSpec((1, d, tn), lambda l, j: (l, 0, j)),
            pl.BlockSpec((1, 1, tn), lambda l, j: (l, 0, j)),
        ],
        out_specs=pl.BlockSpec((1, b, tn), lambda l, j: (l, 0, j)),
        compiler_params=_params("arbitrary", "arbitrary"),
        name="ada_mods",
    )(c, ada_w, ada_b.reshape(depth, 1, n))


def _proj_kernel(x_ref, g_ref, mod_ref, w_ref, o_ref, *, shift_idx, scale_idx, nc):
    h = _norm_mod(x_ref[...], g_ref[...], mod_ref[0, scale_idx:scale_idx + 1, :],
                  mod_ref[0, shift_idx:shift_idx + 1, :]).astype(BF16)
    n = o_ref.shape[1]
    for c0 in range(0, n, nc):
        o_ref[:, c0:c0 + nc] = jnp.dot(h, w_ref[:, c0:c0 + nc],
                                       preferred_element_type=F32).astype(BF16)


def norm_proj(x, g, mod, w, *, seq, shift_idx, scale_idx, tm=512):
    t, d = x.shape
    n = w.shape[1]
    per_seq = seq // tm
    return pl.pallas_call(
        functools.partial(_proj_kernel, shift_idx=shift_idx, scale_idx=scale_idx, nc=1024),
        out_shape=jax.ShapeDtypeStruct((t, n), BF16),
        grid=(t // tm,),
        in_specs=[
            pl.BlockSpec((tm, d), lambda i: (i, 0)),
            pl.BlockSpec((1, d), lambda i: (0, 0)),
            pl.BlockSpec((1, 6, d), lambda i: (i // per_seq, 0, 0)),
            pl.BlockSpec((d, n), lambda i: (0, 0)),
        ],
        out_specs=pl.BlockSpec((tm, n), lambda i: (i, 0)),
        compiler_params=_params("parallel"),
        name="norm_proj",
    )(x, g, mod, w)


def _out_proj_kernel(o_ref, w_ref, x_ref, mod_ref, y_ref, *, gate_idx):
    y = jnp.dot(o_ref[...], w_ref[...], preferred_element_type=F32)
    y_ref[...] = x_ref[...] + mod_ref[0, gate_idx:gate_idx + 1, :] * y


def out_proj_residual(o, w, x, mod, *, seq, gate_idx, tm=1024):
    t, d = x.shape
    k = o.shape[1]
    per_seq = seq // tm
    return pl.pallas_call(
        functools.partial(_out_proj_kernel, gate_idx=gate_idx),
        out_shape=jax.ShapeDtypeStruct((t, d), F32),
        grid=(t // tm,),
        in_specs=[
            pl.BlockSpec((tm, k), lambda i: (i, 0)),
            pl.BlockSpec((k, d), lambda i: (0, 0)),
            pl.BlockSpec((tm, d), lambda i: (i, 0)),
            pl.BlockSpec((1, 6, d), lambda i: (i // per_seq, 0, 0)),
        ],
        out_specs=pl.BlockSpec((tm, d), lambda i: (i, 0)),
        compiler_params=_params("parallel"),
        name="out_proj_residual",
    )(o, w, x, mod)


def _stack_heads(q, qs_ref):
    tq = q.shape[0]
    first = lax.broadcasted_iota(jnp.int32, q.shape, 1) < HEAD_DIM
    qs_ref[0:tq, :] = jnp.where(first, q, 0.0).astype(BF16)
    qs_ref[tq:2 * tq, :] = jnp.where(first, 0.0, q).astype(BF16)


def _unstack_heads(res, tq):
    first = lax.broadcasted_iota(jnp.int32, (tq, LANES), 1) < HEAD_DIM
    return jnp.where(first, res[0:tq, :], res[tq:2 * tq, :])


def _sb_kernel(q_ref, k_ref, v_ref, qg_ref, kg_ref, tri_ref, o_ref,
               kn_ref, qs_ref, acc_ref, carry_ref, *, tq, tk, scale):
    i = pl.program_id(2)
    gp = qs_ref.shape[0]

    @pl.when(i == 0)
    def _():
        for g in range(gp):
            cols = slice(g * LANES, (g + 1) * LANES)
            kn_ref[:, cols] = _head_rms_norm(k_ref[:, cols].astype(F32), kg_ref[...]).astype(BF16)

    for g in range(gp):
        cols = slice(g * LANES, (g + 1) * LANES)
        _stack_heads(_head_rms_norm(q_ref[:, cols].astype(F32), qg_ref[...]) * scale, qs_ref.at[g])
    acc_ref[...] = jnp.zeros_like(acc_ref)
    carry_ref[...] = jnp.zeros_like(carry_ref)

    def block(j, diag):
        k0 = pl.multiple_of(j * tk, tk)
        if diag:
            row = lax.broadcasted_iota(jnp.int32, (2 * tq, tk), 0)
            qpos = jnp.where(row >= tq, row - tq, row)
            strict = lax.broadcasted_iota(jnp.int32, (2 * tq, tk), 1) < qpos
        for g in range(gp):
            cols = slice(g * LANES, (g + 1) * LANES)
            z = lax.dot_general(qs_ref[g], kn_ref[pl.ds(k0, tk), cols], (((1,), (1,)), ((), ())),
                                preferred_element_type=F32)
            neg = jnp.minimum(z, 0.0)
            log_beta = neg - jnp.log2(1.0 + jnp.exp2(neg + neg - z))
            log_fail = log_beta - z
            if diag:
                log_fail = jnp.where(strict, log_fail, 0.0)
            after = jnp.dot(log_fail.astype(BF16), tri_ref[...], preferred_element_type=F32)
            carry = carry_ref[g]
            a = jnp.exp2(log_beta + after + carry)
            if diag:
                a = jnp.where(strict, a, 0.0)
            acc_ref[g] += jnp.dot(a.astype(BF16), v_ref[pl.ds(k0, tk), cols],
                                  preferred_element_type=F32)
            carry_ref[g] = carry + jnp.sum(log_fail, axis=-1, keepdims=True)

    block(i, True)

    def more(t):
        worst = carry_ref[0]
        for g in range(1, gp):
            worst = jnp.maximum(worst, carry_ref[g])
        return jnp.logical_and(t < i, jnp.max(worst) > SB_DEAD_LOG2)

    def body(t):
        block(i - 1 - t, False)
        return t + 1

    lax.while_loop(more, body, 0)
    for g in range(gp):
        o_ref[:, g * LANES:(g + 1) * LANES] = _unstack_heads(acc_ref[g], tq).astype(BF16)


def sb_attention(qkv, q_g, k_g, *, batch, seq, tq=256, gp=8):
    t, n3 = qkv.shape
    d = n3 // 3
    groups = d // LANES // gp
    gw = gp * LANES
    tk = tq
    nq = seq // tq
    idx = jnp.arange(tk)
    tri = (idx[:, None] > idx[None, :]).astype(BF16)
    qg2 = jnp.concatenate([q_g, q_g]).reshape(1, LANES)
    kg2 = jnp.concatenate([k_g, k_g]).reshape(1, LANES)
    return pl.pallas_call(
        functools.partial(_sb_kernel, tq=tq, tk=tk, scale=HEAD_DIM ** -0.5 * LOG2_E),
        out_shape=jax.ShapeDtypeStruct((t, d), BF16),
        grid=(batch, groups, nq),
        in_specs=[
            pl.BlockSpec((tq, gw), lambda b, p, i: (b * nq + i, p)),
            pl.BlockSpec((seq, gw), lambda b, p, i: (b, groups + p)),
            pl.BlockSpec((seq, gw), lambda b, p, i: (b, 2 * groups + p)),
            pl.BlockSpec((1, LANES), lambda b, p, i: (0, 0)),
            pl.BlockSpec((1, LANES), lambda b, p, i: (0, 0)),
            pl.BlockSpec((tk, tk), lambda b, p, i: (0, 0)),
        ],
        out_specs=pl.BlockSpec((tq, gw), lambda b, p, i: (b * nq + i, p)),
        scratch_shapes=[
            pltpu.VMEM((seq, gw), BF16),
            pltpu.VMEM((gp, 2 * tq, LANES), BF16),
            pltpu.VMEM((gp, 2 * tq, LANES), F32),
            pltpu.VMEM((gp, 2 * tq, 1), F32),
        ],
        compiler_params=_params("parallel", "parallel", "arbitrary"),
        name="sb_attention",
    )(qkv, qkv, qkv, qg2, kg2, tri)


def _ca_kernel(q_ref, k_ref, v_ref, qg_ref, kg_ref, bias_ref, o_ref, kn_ref, qs_ref,
               *, tq, nblk, scale):
    i = pl.program_id(2)
    gp = qs_ref.shape[0]

    @pl.when(i == 0)
    def _():
        for g in range(gp):
            cols = slice(g * LANES, (g + 1) * LANES)
            kn_ref[:, cols] = _head_rms_norm(k_ref[:, cols].astype(F32), kg_ref[...]).astype(BF16)

    starts, live = [], []
    for o in range(nblk):
        jb = i - (nblk - 1) + o
        starts.append(pl.multiple_of(jnp.maximum(jb, 0) * tq, tq))
        live.append(jb >= 0)

    for g in range(gp):
        cols = slice(g * LANES, (g + 1) * LANES)
        _stack_heads(_head_rms_norm(q_ref[:, cols].astype(F32), qg_ref[...]) * scale, qs_ref.at[g])
        qs = qs_ref[g]
        scores = []
        for o in range(nblk):
            z = lax.dot_general(qs, kn_ref[pl.ds(starts[o], tq), cols], (((1,), (1,)), ((), ())),
                                preferred_element_type=F32)
            z = z + bias_ref[g, :, o * tq:(o + 1) * tq]
            scores.append(jnp.where(live[o], z, NEG_INF))
        m_lanes = scores[0]
        for o in range(1, nblk):
            m_lanes = jnp.maximum(m_lanes, scores[o])
        m = jnp.max(m_lanes, axis=-1, keepdims=True)
        l_lanes = jnp.zeros((2 * tq, tq), F32)
        acc = jnp.zeros((2 * tq, LANES), F32)
        for o in range(nblk):
            p = jnp.exp2(scores[o] - m)
            l_lanes = l_lanes + p
            acc = acc + jnp.dot(p.astype(BF16), v_ref[pl.ds(starts[o], tq), cols],
                                preferred_element_type=F32)
        l = jnp.sum(l_lanes, axis=-1, keepdims=True)
        o_ref[:, cols] = _unstack_heads(acc / l, tq).astype(BF16)


def ca_attention(qkv, q_g, k_g, rel_bias, *, batch, seq, tq=128, gp=8):
    t, n3 = qkv.shape
    d = n3 // 3
    pairs = d // LANES
    groups = pairs // gp
    gw = gp * LANES
    nq = seq // tq
    pad = CA_PREV * CHUNK
    nblk = pad // tq + 1
    band = nblk * tq
    heads = rel_bias.shape[0]
    span = tq + band - 1
    rel_of_m = pad - ((jnp.arange(span) + tq - 1) % span - (tq - 1))
    row = rel_bias[:, jnp.clip(rel_of_m, -REL_CLIP, REL_CLIP) + REL_CLIP].astype(F32)
    bias = jnp.tile(row, (1, tq))[:, :tq * (span - 1)].reshape(heads, tq, span - 1)[:, :, :band]
    qi = jnp.arange(tq)[:, None]
    kj = jnp.arange(band)[None, :]
    qc = qi // CHUNK
    kc = (kj - pad) // CHUNK
    visible = (kc <= qc) & (kc >= qc - CA_PREV)
    bias = jnp.where(visible[None], bias * LOG2_E, NEG_INF).reshape(heads // 2, 2 * tq, band)
    qg2 = jnp.concatenate([q_g, q_g]).reshape(1, LANES)
    kg2 = jnp.concatenate([k_g, k_g]).reshape(1, LANES)
    return pl.pallas_call(
        functools.partial(_ca_kernel, tq=tq, nblk=nblk, scale=HEAD_DIM ** -0.5 * LOG2_E),
        out_shape=jax.ShapeDtypeStruct((t, d), BF16),
        grid=(batch, groups, nq),
        in_specs=[
            pl.BlockSpec((tq, gw), lambda b, p, i: (b * nq + i, p)),
            pl.BlockSpec((seq, gw), lambda b, p, i: (b, groups + p)),
            pl.BlockSpec((seq, gw), lambda b, p, i: (b, 2 * groups + p)),
            pl.BlockSpec((1, LANES), lambda b, p, i: (0, 0)),
            pl.BlockSpec((1, LANES), lambda b, p, i: (0, 0)),
            pl.BlockSpec((gp, 2 * tq, band), lambda b, p, i: (p, 0, 0)),
        ],
        out_specs=pl.BlockSpec((tq, gw), lambda b, p, i: (b * nq + i, p)),
        scratch_shapes=[pltpu.VMEM((seq, gw), BF16), pltpu.VMEM((gp, 2 * tq, LANES), BF16)],
        compiler_params=_params("parallel", "parallel", "arbitrary"),
        name="ca_attention",
    )(qkv, qkv, qkv, qg2, kg2, bias)


def _sgu_in_kernel(x_ref, g_ref, mod_ref, w_ref, vg_ref, o_ref, h_ref, z_ref, *, nc):
    j = pl.program_id(1)
    n = o_ref.shape[1]

    @pl.when(j == 0)
    def _():
        h_ref[...] = _norm_mod(x_ref[...], g_ref[...], mod_ref[0, 1:2, :],
                               mod_ref[0, 0:1, :]).astype(BF16)
        for c0 in range(0, n, nc):
            acc = jnp.dot(h_ref[...], w_ref[:, c0:c0 + nc], preferred_element_type=F32)
            o_ref[:, c0:c0 + nc] = jax.nn.gelu(acc).astype(BF16)

    @pl.when(j == 1)
    def _():
        ss = jnp.zeros((o_ref.shape[0], 1), F32)
        for c0 in range(0, n, nc):
            acc = jnp.dot(h_ref[...], w_ref[:, c0:c0 + nc], preferred_element_type=F32)
            z = jax.nn.gelu(acc)
            z_ref[:, c0:c0 + nc] = z
            ss = ss + jnp.sum(z * z, axis=-1, keepdims=True)
        rstd = lax.rsqrt(ss * (1.0 / n) + EPS)
        o_ref[...] = (z_ref[...] * rstd * vg_ref[...]).astype(BF16)


def sgu_in(x, g, mod, w_in, v_g, *, seq, tm=512):
    t, d = x.shape
    half = w_in.shape[1] // 2
    per_seq = seq // tm
    return pl.pallas_call(
        functools.partial(_sgu_in_kernel, nc=1024),
        out_shape=jax.ShapeDtypeStruct((t, 2 * half), BF16),
        grid=(t // tm, 2),
        in_specs=[
            pl.BlockSpec((tm, d), lambda i, j: (i, 0)),
            pl.BlockSpec((1, d), lambda i, j: (0, 0)),
            pl.BlockSpec((1, 6, d), lambda i, j: (i // per_seq, 0, 0)),
            pl.BlockSpec((d, half), lambda i, j: (0, j)),
            pl.BlockSpec((1, half), lambda i, j: (0, 0)),
        ],
        out_specs=pl.BlockSpec((tm, half), lambda i, j: (i, j)),
        scratch_shapes=[pltpu.VMEM((tm, d), BF16), pltpu.VMEM((tm, half), F32)],
        compiler_params=_params("parallel", "arbitrary"),
        name="sgu_in",
    )(x, g, mod, w_in, v_g)


def _sgu_out_kernel(u_ref, v_ref, ws_ref, bs_ref, wout_ref, x_ref, mod_ref, o_ref, y_ref,
                    *, groups, gw):
    tm = u_ref.shape[0]
    rows = lax.broadcasted_iota(jnp.int32, (SG_CHUNK, SG_CHUNK), 0)
    cols = lax.broadcasted_iota(jnp.int32, (SG_CHUNK, SG_CHUNK), 1)
    causal = (cols // CHUNK) <= (rows // CHUNK)
    for g in range(groups):
        wg = jnp.where(causal, ws_ref[g], 0.0).astype(BF16)
        c0 = g * gw
        for r0 in range(0, tm, SG_CHUNK):
            mixed = jnp.dot(wg, v_ref[r0:r0 + SG_CHUNK, c0:c0 + gw],
                            preferred_element_type=F32) + bs_ref[g]
            u = u_ref[r0:r0 + SG_CHUNK, c0:c0 + gw].astype(F32)
            y_ref[r0:r0 + SG_CHUNK, c0:c0 + gw] = (u * mixed).astype(BF16)
    y = jnp.dot(y_ref[...], wout_ref[...], preferred_element_type=F32)
    o_ref[...] = x_ref[...] + mod_ref[0, 2:3, :] * y


def sgu_out(z, w_s, b_s, w_out, x, mod, *, seq, tm=256):
    t, d = x.shape
    half = z.shape[1] // 2
    groups = w_s.shape[0]
    per_seq = seq // tm
    return pl.pallas_call(
        functools.partial(_sgu_out_kernel, groups=groups, gw=half // groups),
        out_shape=jax.ShapeDtypeStruct((t, d), F32),
        grid=(t // tm,),
        in_specs=[
            pl.BlockSpec((tm, half), lambda i: (i, 0)),
            pl.BlockSpec((tm, half), lambda i: (i, 1)),
            pl.BlockSpec((groups, SG_CHUNK, SG_CHUNK), lambda i: (0, 0, 0)),
            pl.BlockSpec((groups, SG_CHUNK, 1), lambda i: (0, 0, 0)),
            pl.BlockSpec((half, d), lambda i: (0, 0)),
            pl.BlockSpec((tm, d), lambda i: (i, 0)),
            pl.BlockSpec((1, 6, d), lambda i: (i // per_seq, 0, 0)),
        ],
        out_specs=pl.BlockSpec((tm, d), lambda i: (i, 0)),
        scratch_shapes=[pltpu.VMEM((tm, half), BF16)],
        compiler_params=_params("parallel"),
        name="sgu_out",
    )(z, z, w_s, b_s.reshape(groups, SG_CHUNK, 1), w_out, x, mod)


def _ffn_kernel(x_ref, g_ref, mod_ref, w1_ref, w3_ref, w2_ref, o_ref, h_ref, acc_ref):
    j = pl.program_id(1)

    @pl.when(j == 0)
    def _():
        h_ref[...] = _norm_mod(x_ref[...], g_ref[...], mod_ref[0, 4:5, :],
                               mod_ref[0, 3:4, :]).astype(BF16)
        acc_ref[...] = jnp.zeros_like(acc_ref)

    h = h_ref[...]
    a = jnp.dot(h, w1_ref[...], preferred_element_type=F32)
    b = jnp.dot(h, w3_ref[...], preferred_element_type=F32)
    act = (a * jax.nn.sigmoid(a) * b).astype(BF16)
    acc_ref[...] += jnp.dot(act, w2_ref[...], preferred_element_type=F32)

    @pl.when(j == pl.num_programs(1) - 1)
    def _():
        o_ref[...] = x_ref[...] + mod_ref[0, 5:6, :] * acc_ref[...]


def ffn_dense(x, g, mod, w13, w2, *, seq, tm=512):
    t, d = x.shape
    f = w2.shape[0]
    tf, nf = f, 1
    per_seq = seq // tm
    once = pl.Buffered(1)
    return pl.pallas_call(
        _ffn_kernel,
        out_shape=jax.ShapeDtypeStruct((t, d), F32),
        grid=(t // tm, nf),
        in_specs=[
            pl.BlockSpec((tm, d), lambda i, j: (i, 0)),
            pl.BlockSpec((1, d), lambda i, j: (0, 0)),
            pl.BlockSpec((1, 6, d), lambda i, j: (i // per_seq, 0, 0)),
            pl.BlockSpec((d, tf), lambda i, j: (0, j), pipeline_mode=once),
            pl.BlockSpec((d, tf), lambda i, j: (0, nf + j), pipeline_mode=once),
            pl.BlockSpec((tf, d), lambda i, j: (j, 0), pipeline_mode=once),
        ],
        out_specs=pl.BlockSpec((tm, d), lambda i, j: (i, 0)),
        scratch_shapes=[pltpu.VMEM((tm, d), BF16), pltpu.VMEM((tm, d), F32)],
        compiler_params=_params("parallel", "arbitrary"),
        name="ffn_dense",
    )(x, g, mod, w13, w13, w2)


def _router_kernel(x_ref, g_ref, mod_ref, wr_ref, br_ref, h_ref, sel_ref):
    h = _norm_mod(x_ref[...], g_ref[...], mod_ref[0, 4:5, :], mod_ref[0, 3:4, :])
    h_ref[...] = h
    logits = jnp.dot(h, wr_ref[...], precision=lax.Precision.HIGHEST,
                     preferred_element_type=F32) + br_ref[...]
    lane = lax.broadcasted_iota(jnp.int32, logits.shape, 1).astype(F32)
    m1 = jnp.max(logits, axis=-1, keepdims=True)
    i1 = jnp.min(jnp.where(logits == m1, lane, float(LANES)), axis=-1, keepdims=True)
    rest = jnp.where(lane == i1, NEG_INF, logits)
    m2 = jnp.max(rest, axis=-1, keepdims=True)
    i2 = jnp.min(jnp.where(rest == m2, lane, float(LANES)), axis=-1, keepdims=True)
    e = jnp.exp(m2 - m1)
    w1 = 1.0 / (1.0 + e)
    sel_ref[...] = (jnp.where(lane == 0.0, i1, 0.0) + jnp.where(lane == 1.0, i2, 0.0)
                    + jnp.where(lane == 2.0, w1, 0.0) + jnp.where(lane == 3.0, e * w1, 0.0))


def router(x, g, mod, w_r, b_r, *, seq, tm=1024):
    t, d = x.shape
    ne = w_r.shape[1]
    wr = jnp.zeros((d, LANES), F32).at[:, :ne].set(w_r)
    br = jnp.full((1, LANES), NEG_INF, F32).at[0, :ne].set(b_r)
    per_seq = seq // tm
    return pl.pallas_call(
        _router_kernel,
        out_shape=(jax.ShapeDtypeStruct((t, d), F32), jax.ShapeDtypeStruct((t, LANES), F32)),
        grid=(t // tm,),
        in_specs=[
            pl.BlockSpec((tm, d), lambda i: (i, 0)),
            pl.BlockSpec((1, d), lambda i: (0, 0)),
            pl.BlockSpec((1, 6, d), lambda i: (i // per_seq, 0, 0)),
            pl.BlockSpec((d, LANES), lambda i: (0, 0)),
            pl.BlockSpec((1, LANES), lambda i: (0, 0)),
        ],
        out_specs=(pl.BlockSpec((tm, d), lambda i: (i, 0)),
                   pl.BlockSpec((tm, LANES), lambda i: (i, 0))),
        compiler_params=_params("parallel"),
        name="router",
    )(x, g, mod, wr, br)


def dispatch_plan(sel, *, tm):
    t = sel.shape[0]
    ntiles = 2 * t // tm + N_EXPERTS
    experts = sel[:, :2].astype(jnp.int32)
    onehot = (experts[:, :, None] == jnp.arange(N_EXPERTS)[None, None, :]).astype(jnp.int32)
    per_tok = onehot.sum(axis=1)
    before = jnp.cumsum(per_tok, axis=0) - per_tok
    counts = per_tok.sum(axis=0)
    padded = (counts + tm - 1) // tm * tm
    ends = jnp.cumsum(padded)
    starts = ends - padded
    dest = jnp.take_along_axis(starts[None, :] + before, experts, axis=1)
    tokens = jnp.broadcast_to(jnp.arange(t, dtype=jnp.int32)[:, None], (t, 2))
    src = jnp.zeros((ntiles * tm,), jnp.int32).at[dest.reshape(-1)].set(
        tokens.reshape(-1), unique_indices=True, mode="promise_in_bounds")
    tile_start = jnp.arange(ntiles, dtype=jnp.int32) * tm
    tile_expert = jnp.minimum((tile_start[:, None] >= ends[None, :]).sum(axis=1),
                              N_EXPERTS - 1).astype(jnp.int32)
    tile_valid = (tile_start < ends[-1]).astype(jnp.int32)
    return dest, src, tile_expert, tile_valid


def _row_copy(src_hbm, row, dst_vmem, k, sem):
    return pltpu.make_async_copy(src_hbm.at[pl.ds(row, 1)], dst_vmem.at[pl.ds(k, 1)], sem)


def _start_rows(idx_ref, base, n, src_hbm, dst_vmem, sem):
    for k in range(n):
        _row_copy(src_hbm, idx_ref[0, 0, base + k], dst_vmem, k, sem).start(priority=1)


def _wait_rows(n, src_hbm, dst_vmem, sem):
    pltpu.make_async_copy(src_hbm.at[pl.ds(0, n)], dst_vmem.at[pl.ds(0, n)], sem).wait()


def _gather_rows(idx_ref, base, n, src_hbm, dst_vmem, sem):
    _start_rows(idx_ref, base, n, src_hbm, dst_vmem, sem)
    _wait_rows(n, src_hbm, dst_vmem, sem)


def _moe_kernel(te_ref, tv_ref, src_ref, nxt_ref, h_hbm, w1_ref, w3_ref, w2_ref, o_ref,
                xg_ref, xb_ref, acc_ref, sem):
    r = pl.program_id(0)
    j = pl.program_id(1)
    last_j = pl.num_programs(1) - 1
    tm = xb_ref.shape[0]
    live = tv_ref[r] == 1
    nxt = jnp.minimum(r + 1, pl.num_programs(0) - 1)
    nxt_live = (r + 1 < pl.num_programs(0)) & (tv_ref[nxt] == 1)

    for slot in range(2):
        @pl.when(live & (j == 0) & (r % 2 == slot))
        def _(slot=slot):
            if slot == 0:
                @pl.when(r == 0)
                def _():
                    _start_rows(src_ref, 0, tm, h_hbm, xg_ref.at[0], sem.at[0])
            _wait_rows(tm, h_hbm, xg_ref.at[slot], sem.at[slot])
            xb_ref[...] = xg_ref[slot].astype(BF16)
            acc_ref[...] = jnp.zeros_like(acc_ref)

        @pl.when(live & nxt_live & (j == last_j) & (r % 2 == slot))
        def _(slot=slot):
            _start_rows(nxt_ref, 0, tm, h_hbm, xg_ref.at[1 - slot], sem.at[1 - slot])

    @pl.when(live)
    def _():
        xb = xb_ref[...]
        a = jnp.dot(xb, w1_ref[0, 0], preferred_element_type=F32)
        b = jnp.dot(xb, w3_ref[0, 0], preferred_element_type=F32)
        act = (a * jax.nn.sigmoid(a) * b).astype(BF16)
        acc_ref[...] += jnp.dot(act, w2_ref[0, 0], preferred_element_type=F32)

    @pl.when(j == last_j)
    def _():
        @pl.when(live)
        def _():
            o_ref[...] = acc_ref[...]

        @pl.when(jnp.logical_not(live))
        def _():
            o_ref[...] = jnp.zeros_like(o_ref)


def moe_experts(h, src, tile_expert, tile_valid, w1, w3, w2, *, layer, tm, tf=1792):
    t, d = h.shape
    f = w1.shape[-1]
    ntiles = tile_expert.shape[0]
    nf = f // tf
    src3 = src.reshape(ntiles, 1, tm)

    def wcol(r, j, te, tv):
        return jnp.where(tv[r] == 1, j, nf - 1)

    return pl.pallas_call(
        _moe_kernel,
        out_shape=jax.ShapeDtypeStruct((ntiles * tm, d), F32),
        grid_spec=pltpu.PrefetchScalarGridSpec(
            num_scalar_prefetch=2,
            grid=(ntiles, nf),
            in_specs=[
                pl.BlockSpec((1, 1, tm), lambda r, j, te, tv: (r, 0, 0), memory_space=pltpu.SMEM),
                pl.BlockSpec((1, 1, tm), lambda r, j, te, tv: (jnp.minimum(r + 1, ntiles - 1), 0, 0),
                             memory_space=pltpu.SMEM),
                pl.BlockSpec(memory_space=pl.ANY),
                pl.BlockSpec((1, 1, d, tf),
                             lambda r, j, te, tv: (layer, te[r], 0, wcol(r, j, te, tv))),
                pl.BlockSpec((1, 1, d, tf),
                             lambda r, j, te, tv: (layer, te[r], 0, wcol(r, j, te, tv))),
                pl.BlockSpec((1, 1, tf, d),
                             lambda r, j, te, tv: (layer, te[r], wcol(r, j, te, tv), 0)),
            ],
            out_specs=pl.BlockSpec((tm, d), lambda r, j, te, tv: (r, 0)),
            scratch_shapes=[
                pltpu.VMEM((2, tm, d), F32),
                pltpu.VMEM((tm, d), BF16),
                pltpu.VMEM((tm, d), F32),
                pltpu.SemaphoreType.DMA((2,)),
            ],
        ),
        compiler_params=_params("arbitrary", "arbitrary"),
        name="moe_experts",
    )(tile_expert, tile_valid, src3, src3, h, w1, w3, w2)


def _combine_kernel(dest_ref, y_hbm, sel_ref, x_ref, mod_ref, o_ref, yg_ref, sem):
    tc = x_ref.shape[0]
    _gather_rows(dest_ref, 0, tc, y_hbm, yg_ref.at[0], sem)
    _gather_rows(dest_ref, tc, tc, y_hbm, yg_ref.at[1], sem)
    sel = sel_ref[...]
    lane = lax.broadcasted_iota(jnp.int32, sel.shape, 1)
    w_first = jnp.sum(jnp.where(lane == 2, sel, 0.0), axis=-1, keepdims=True)
    w_second = jnp.sum(jnp.where(lane == 3, sel, 0.0), axis=-1, keepdims=True)
    y = w_first * yg_ref[0] + w_second * yg_ref[1]
    o_ref[...] = x_ref[...] + mod_ref[0, 5:6, :] * y


def moe_combine(ys, dest, sel, x, mod, *, seq, tc=512):
    t, d = x.shape
    nblk = t // tc
    per_seq = seq // tc
    idx = dest.reshape(nblk, tc, 2).transpose(0, 2, 1).reshape(nblk, 1, 2 * tc)
    return pl.pallas_call(
        _combine_kernel,
        out_shape=jax.ShapeDtypeStruct((t, d), F32),
        grid=(nblk,),
        in_specs=[
            pl.BlockSpec((1, 1, 2 * tc), lambda i: (i, 0, 0), memory_space=pltpu.SMEM),
            pl.BlockSpec(memory_space=pl.ANY),
            pl.BlockSpec((tc, LANES), lambda i: (i, 0)),
            pl.BlockSpec((tc, d), lambda i: (i, 0)),
            pl.BlockSpec((1, 6, d), lambda i: (i // per_seq, 0, 0)),
        ],
        out_specs=pl.BlockSpec((tc, d), lambda i: (i, 0)),
        scratch_shapes=[pltpu.VMEM((2, tc, d), F32), pltpu.SemaphoreType.DMA],
        compiler_params=_params("arbitrary"),
        name="moe_combine",
    )(idx, ys, sel, x, mod)


def kernel(x, c, norm_g, ada_w, ada_b, sb_wqkv, sb_qg, sb_kg, sb_wo, sg_win, sg_vg, sg_ws, sg_bs,
           sg_wout, ca_wqkv, ca_qg, ca_kg, ca_relb, ca_wo, ff_w13, ff_w2, moe_wr, moe_br,
           moe_w1, moe_w3, moe_w2):
    batch, seq, d = x.shape
    depth = norm_g.shape[0]
    mods = ada_mods(c, ada_w, ada_b).reshape(depth, batch, 6, d)
    xt = x.reshape(batch * seq, d)
    bf = lambda w: w.astype(BF16)
    moe_w1b, moe_w3b, moe_w2b = bf(moe_w1), bf(moe_w3), bf(moe_w2)
    for i in range(depth):
        mod = mods[i]
        kind, j = i % 3, i // 3
        g_tok = norm_g[i, 0].reshape(1, d)
        g_ch = norm_g[i, 1].reshape(1, d)
        if kind == 0:
            qkv = norm_proj(xt, g_tok, mod, bf(sb_wqkv[j]), seq=seq, shift_idx=0, scale_idx=1)
            o = sb_attention(qkv, sb_qg[j], sb_kg[j], batch=batch, seq=seq)
            xt = out_proj_residual(o, bf(sb_wo[j]), xt, mod, seq=seq, gate_idx=2)
        elif kind == 1:
            z = sgu_in(xt, g_tok, mod, bf(sg_win[j]), sg_vg[j].reshape(1, -1), seq=seq)
            xt = sgu_out(z, sg_ws[j], sg_bs[j], bf(sg_wout[j]), xt, mod, seq=seq)
        else:
            qkv = norm_proj(xt, g_tok, mod, bf(ca_wqkv[j]), seq=seq, shift_idx=0, scale_idx=1)
            o = ca_attention(qkv, ca_qg[j], ca_kg[j], ca_relb[j], batch=batch, seq=seq)
            xt = out_proj_residual(o, bf(ca_wo[j]), xt, mod, seq=seq, gate_idx=2)
        m = i // 2
        if i % 2 == 0:
            xt = ffn_dense(xt, g_ch, mod, bf(ff_w13[m]), bf(ff_w2[m]), seq=seq)
        else:
            h, sel = router(xt, g_ch, mod, moe_wr[m], moe_br[m], seq=seq)
            dest, src, tile_expert, tile_valid = dispatch_plan(sel, tm=MOE_TM)
            ys = moe_experts(h, src, tile_expert, tile_valid, moe_w1b, moe_w3b, moe_w2b,
                             layer=m, tm=MOE_TM)
            xt = moe_combine(ys, dest, sel, xt, mod, seq=seq)
    return xt.reshape(batch, seq, d)
```
